```python
import math
import jax
import jax.numpy as jnp
from jax import lax
import numpy as np

D_MODEL = 2048
BATCH = 1
SEQ = 8192
DEPTH = 4

MEM_LEN = 256
EPS = 1e-6
GDN_HEADS = 8
GDN_DK = 128
GDN_DV = 128
GDN_CONV = 4
GDN_CHUNK = 64
LRU_WIDTH = D_MODEL // 2
LRU_BLOCKS = 8
LRU_BLOCK = LRU_WIDTH // LRU_BLOCKS
LRU_CONV = 4
LRU_C = 8.0
SWA_HEADS = 32
SWA_KV_HEADS = 8
SWA_GROUP = SWA_HEADS // SWA_KV_HEADS
SWA_HD = 64
SWA_WINDOW = 128
SWA_BLOCK = 128
ROPE_THETA = 10000.0
X_HEADS = 4
X_HD = 128
D_FF = 5632
FFN_CONV = 3

GDN_QK = GDN_HEADS * GDN_DK
GDN_V = GDN_HEADS * GDN_DV
GDN_QKV = 2 * GDN_QK + GDN_V
HYB_SIZES = (GDN_QKV, GDN_V, GDN_HEADS, GDN_HEADS, LRU_WIDTH, LRU_WIDTH)
HYB_IN = GDN_QKV + GDN_V + 2 * GDN_HEADS + 2 * LRU_WIDTH
HYB_MIX = GDN_V + LRU_WIDTH
SWA_Q = SWA_HEADS * SWA_HD
SWA_KV = SWA_KV_HEADS * SWA_HD
SWA_SIZES = (SWA_Q, SWA_KV, SWA_KV)
SWA_IN = SWA_Q + 2 * SWA_KV
X_INNER = X_HEADS * X_HD
N_EVEN = (DEPTH + 1) // 2
N_ODD = DEPTH // 2

kernel_name = 'hybrid_gdn_rglru_swa_trunk'


def split_cols(t, sizes):
    out, start = [], 0
    for s in sizes:
        out.append(t[..., start:start + s])
        start += s
    return out


def rmsnorm(x, g):
    xf = x.astype(jnp.float32)
    y = xf * lax.rsqrt(jnp.mean(xf * xf, axis=-1, keepdims=True) + EPS)
    return (y * g.astype(jnp.float32)).astype(x.dtype)


def causal_dwconv(x, w):
    width, ch = w.shape
    return lax.conv_general_dilated(
        x, w[:, None, :].astype(x.dtype), window_strides=(1,), padding=((width - 1, 0),),
        dimension_numbers=('NWC', 'WIO', 'NWC'), feature_group_count=ch)


def rope_tables(positions, dim):
    inv = 1.0 / (ROPE_THETA ** (jnp.arange(0, dim, 2, dtype=jnp.float32) / dim))
    ang = positions.astype(jnp.float32)[..., None] * inv
    return jnp.cos(ang), jnp.sin(ang)


def apply_rope(x, cos, sin):
    xf = x.astype(jnp.float32)
    x1, x2 = jnp.split(xf, 2, axis=-1)
    c, s = cos[:, :, None, :], sin[:, :, None, :]
    return jnp.concatenate([x1 * c - x2 * s, x2 * c + x1 * s], axis=-1).astype(x.dtype)


def gated_delta_rule(q, k, v, g, beta):
    b, s, h, dk = q.shape
    dv = v.shape[-1]
    c = GDN_CHUNK
    n = s // c

    def blocks(t):
        t = t.astype(jnp.float32).reshape((b, n, c, h) + t.shape[3:])
        return jnp.moveaxis(t, 3, 1)

    q = blocks(q) * (dk ** -0.5)
    k, v, g, beta = blocks(k), blocks(v), blocks(g), blocks(beta)
    gc = jnp.cumsum(g, axis=-1)
    causal = jnp.tril(jnp.ones((c, c), dtype=bool))
    strict = jnp.tril(jnp.ones((c, c), dtype=bool), -1)
    decay = jnp.exp(jnp.where(causal, gc[..., :, None] - gc[..., None, :], -jnp.inf))
    kb = k * beta[..., None]
    l_mat = jnp.where(strict, jnp.einsum('bhnik,bhnjk->bhnij', kb, k) * decay, 0.0)
    eye = jnp.eye(c, dtype=jnp.float32)
    tmat = lax.linalg.triangular_solve(eye + l_mat, jnp.broadcast_to(eye, l_mat.shape),
                                       left_side=True, lower=True, unit_diagonal=True)
    u = jnp.einsum('bhnij,bhnjd->bhnid', tmat, v * beta[..., None])
    w = jnp.einsum('bhnij,bhnjd->bhnid', tmat, kb * jnp.exp(gc)[..., None])
    attn = jnp.where(causal, jnp.einsum('bhnik,bhnjk->bhnij', q, k) * decay, 0.0)
    q_dec = q * jnp.exp(gc)[..., None]
    g_last = gc[..., -1]
    k_dec = k * jnp.exp(g_last[..., None] - gc)[..., None]

    def step(state, inp):
        u_i, w_i, a_i, qd_i, kd_i, gl_i = inp
        v_new = u_i - jnp.einsum('bhck,bhkv->bhcv', w_i, state)
        o_i = jnp.einsum('bhck,bhkv->bhcv', qd_i, state) + jnp.einsum('bhcj,bhjv->bhcv', a_i, v_new)
        state = state * jnp.exp(gl_i)[..., None, None] + jnp.einsum('bhck,bhcv->bhkv', kd_i, v_new)
        return state, o_i

    xs = tuple(jnp.moveaxis(t, 2, 0) for t in (u, w, attn, q_dec, k_dec, g_last))
    state0 = jnp.zeros((b, h, dk, dv), jnp.float32)
    _, o = lax.scan(step, state0, xs)
    return jnp.transpose(o, (1, 0, 3, 2, 4)).reshape(b, s, h, dv)


def gdn_mixer(qkv, z, b_raw, a_raw, conv_w, a_log, dt_bias, norm_g):
    bsz, s, _ = qkv.shape
    qkv_c = jax.nn.silu(causal_dwconv(qkv, conv_w))
    q, k, v = split_cols(qkv_c, (GDN_QK, GDN_QK, GDN_V))

    def heads(t, d):
        return t.reshape(bsz, s, GDN_HEADS, d).astype(jnp.float32)

    def l2n(t):
        return t * lax.rsqrt(jnp.sum(t * t, axis=-1, keepdims=True) + EPS)

    q, k, v = l2n(heads(q, GDN_DK)), l2n(heads(k, GDN_DK)), heads(v, GDN_DV)
    beta = jax.nn.sigmoid(b_raw.astype(jnp.float32))
    g = -jnp.exp(a_log.astype(jnp.float32)) * jax.nn.softplus(
        a_raw.astype(jnp.float32) + dt_bias.astype(jnp.float32))
    o = gated_delta_rule(q, k, v, g, beta)
    o = rmsnorm(o, norm_g) * jax.nn.silu(heads(z, GDN_DV))
    return o.reshape(bsz, s, GDN_V).astype(qkv.dtype)


def rglru_mixer(xr, gate, conv_w, conv_b, w_r, b_r, w_i, b_i, lam):
    bsz, s, _ = xr.shape
    xc = (causal_dwconv(xr, conv_w) + conv_b).astype(jnp.float32)
    xh = xc.reshape(bsz, s, LRU_BLOCKS, LRU_BLOCK)
    r = jax.nn.sigmoid(jnp.einsum('bshi,hij->bshj', xh, w_r.astype(jnp.float32)).reshape(bsz, s, LRU_WIDTH)
                       + b_r.astype(jnp.float32))
    i = jax.nn.sigmoid(jnp.einsum('bshi,hij->bshj', xh, w_i.astype(jnp.float32)).reshape(bsz, s, LRU_WIDTH)
                       + b_i.astype(jnp.float32))
    log_a = -LRU_C * r * jax.nn.softplus(-lam.astype(jnp.float32))
    a = jnp.exp(log_a)
    inp = jnp.sqrt(-jnp.expm1(2.0 * log_a)) * (i * xc)

    def combine(lhs, rhs):
        a_l, h_l = lhs
        a_r, h_r = rhs
        return a_l * a_r, a_r * h_l + h_r

    _, hseq = lax.associative_scan(combine, (a, inp), axis=1)
    return hseq.astype(xr.dtype) * jax.nn.gelu(gate)


def swa_mixer(qkv, q_norm, k_norm, sinks, cos, sin):
    bsz, s, _ = qkv.shape
    q, k, v = split_cols(qkv, SWA_SIZES)
    q = apply_rope(rmsnorm(q.reshape(bsz, s, SWA_HEADS, SWA_HD), q_norm), cos, sin)
    k = apply_rope(rmsnorm(k.reshape(bsz, s, SWA_KV_HEADS, SWA_HD), k_norm), cos, sin)
    v = v.reshape(bsz, s, SWA_KV_HEADS, SWA_HD)
    blk = SWA_BLOCK
    nb = s // blk
    qb = q.reshape(bsz, nb, blk, SWA_KV_HEADS, SWA_GROUP, SWA_HD)

    def band(t):
        tp = jnp.pad(t, ((0, 0), (blk, 0), (0, 0), (0, 0))).reshape(bsz, nb + 1, blk, SWA_KV_HEADS, SWA_HD)
        return jnp.concatenate([tp[:, :-1], tp[:, 1:]], axis=2)

    kb, vb = band(k), band(v)
    sc = jnp.einsum('bnqkgd,bnjkd->bnkgqj', qb, kb).astype(jnp.float32) * (SWA_HD ** -0.5)
    qi = jnp.arange(blk)[:, None]
    kj = jnp.arange(2 * blk)[None, :]
    rel = blk + qi - kj
    bidx = jnp.arange(nb)[:, None, None]
    valid = (rel >= 0) & (rel < SWA_WINDOW) & (bidx * blk + kj - blk >= 0)
    sc = jnp.where(valid[None, :, None, None], sc, -jnp.inf)
    sink = sinks.astype(jnp.float32).reshape(SWA_KV_HEADS, SWA_GROUP)[None, None, :, :, None, None]
    m = jnp.maximum(jnp.max(sc, axis=-1, keepdims=True), sink)
    p = jnp.exp(sc - m)
    denom = jnp.sum(p, axis=-1, keepdims=True) + jnp.exp(sink - m)
    o = jnp.einsum('bnkgqj,bnjkd->bnqkgd', p / denom, vb.astype(jnp.float32))
    return o.reshape(bsz, s, SWA_Q).astype(qkv.dtype)


def cross_attn(h, mem_n, wq, wkv, wo, q_norm, k_norm):
    bsz, s, _ = h.shape
    q = rmsnorm((h @ wq).reshape(bsz, s, X_HEADS, X_HD), q_norm)
    k, v = split_cols(mem_n @ wkv, (X_INNER, X_INNER))
    k = rmsnorm(k.reshape(bsz, MEM_LEN, X_HEADS, X_HD), k_norm)
    v = v.reshape(bsz, MEM_LEN, X_HEADS, X_HD)
    sc = jnp.einsum('bshd,bmhd->bhsm', q, k).astype(jnp.float32) * (X_HD ** -0.5)
    p = jax.nn.softmax(sc, axis=-1).astype(v.dtype)
    o = jnp.einsum('bhsm,bmhd->bshd', p, v).reshape(bsz, s, X_INNER)
    return o @ wo


def conv_ffn(h, w_in, conv_w, conv_b, w_out):
    gt, up = split_cols(h @ w_in, (D_FF, D_FF))
    gt = causal_dwconv(gt, conv_w) + conv_b
    return (jax.nn.silu(gt) * up) @ w_out


def setup_inputs(seed: int = 0) -> dict:
    key = jax.random.key(seed)
    ks = iter(jax.random.split(key, 48))
    f32 = jnp.float32
    D, L, E, O = D_MODEL, DEPTH, N_EVEN, N_ODD

    def nrm(shape, scale):
        return jax.random.normal(next(ks), shape, f32) * scale

    def gain(shape):
        return 1.0 + nrm(shape, 0.02)

    x = nrm((BATCH, SEQ, D), 1.0)
    mem = nrm((BATCH, MEM_LEN, D), 1.0)
    positions = (jax.random.randint(next(ks), (BATCH, 1), 0, 1024) + jnp.arange(SEQ)[None, :]).astype(jnp.int32)
    norm_mix = gain((L, D))
    norm_cross = gain((L, D))
    norm_mem = gain((L, D))
    norm_ffn = gain((L, D))
    xq_w = nrm((L, D, X_INNER), D ** -0.5)
    xkv_w = nrm((L, D, 2 * X_INNER), D ** -0.5)
    xo_w = nrm((L, X_INNER, D), X_INNER ** -0.5)
    xq_norm = gain((L, X_HD))
    xk_norm = gain((L, X_HD))
    ffn_in_w = nrm((L, D, 2 * D_FF), D ** -0.5)
    ffn_conv_w = nrm((L, FFN_CONV, D_FF), FFN_CONV ** -0.5)
    ffn_conv_b = nrm((L, D_FF), 0.02)
    ffn_out_w = nrm((L, D_FF, D), D_FF ** -0.5)
    hyb_in_w = nrm((E, D, HYB_IN), D ** -0.5)
    hyb_out_w = nrm((E, HYB_MIX, D), HYB_MIX ** -0.5)
    gdn_conv_w = nrm((E, GDN_CONV, GDN_QKV), GDN_CONV ** -0.5)
    gdn_a_log = jnp.log(jax.random.uniform(next(ks), (E, GDN_HEADS), f32, 1.0, 16.0))
    dt = jnp.exp(jax.random.uniform(next(ks), (E, GDN_HEADS), f32, math.log(1e-3), math.log(1e-1)))
    gdn_dt_bias = dt + jnp.log(-jnp.expm1(-dt))
    gdn_norm = gain((E, GDN_DV))
    lru_conv_w = nrm((E, LRU_CONV, LRU_WIDTH), LRU_CONV ** -0.5)
    lru_conv_b = nrm((E, LRU_WIDTH), 0.02)
    lru_wr = nrm((E, LRU_BLOCKS, LRU_BLOCK, LRU_BLOCK), LRU_BLOCK ** -0.5)
    lru_br = nrm((E, LRU_WIDTH), 0.02)
    lru_wi = nrm((E, LRU_BLOCKS, LRU_BLOCK, LRU_BLOCK), LRU_BLOCK ** -0.5)
    lru_bi = nrm((E, LRU_WIDTH), 0.02)
    a0 = jax.random.uniform(next(ks), (E, LRU_WIDTH), f32, 0.9, 0.999) ** (1.0 / LRU_C)
    lru_lambda = jnp.log(a0) - jnp.log1p(-a0)
    swa_in_w = nrm((O, D, SWA_IN), D ** -0.5)
    swa_out_w = nrm((O, SWA_Q, D), SWA_Q ** -0.5)
    swa_q_norm = gain((O, SWA_HD))
    swa_k_norm = gain((O, SWA_HD))
    swa_sinks = nrm((O, SWA_HEADS), 0.5)
    return {
        'x': x, 'mem': mem, 'positions': positions,
        'norm_mix': norm_mix, 'norm_cross': norm_cross, 'norm_mem': norm_mem, 'norm_ffn': norm_ffn,
        'xq_w': xq_w, 'xkv_w': xkv_w, 'xo_w': xo_w, 'xq_norm': xq_norm, 'xk_norm': xk_norm,
        'ffn_in_w': ffn_in_w, 'ffn_conv_w': ffn_conv_w, 'ffn_conv_b': ffn_conv_b, 'ffn_out_w': ffn_out_w,
        'hyb_in_w': hyb_in_w, 'hyb_out_w': hyb_out_w,
        'gdn_conv_w': gdn_conv_w, 'gdn_a_log': gdn_a_log, 'gdn_dt_bias': gdn_dt_bias, 'gdn_norm': gdn_norm,
        'lru_conv_w': lru_conv_w, 'lru_conv_b': lru_conv_b, 'lru_wr': lru_wr, 'lru_br': lru_br,
        'lru_wi': lru_wi, 'lru_bi': lru_bi, 'lru_lambda': lru_lambda,
        'swa_in_w': swa_in_w, 'swa_out_w': swa_out_w, 'swa_q_norm': swa_q_norm, 'swa_k_norm': swa_k_norm,
        'swa_sinks': swa_sinks,
    }


def reference(x, mem, positions, norm_mix, norm_cross, norm_mem, norm_ffn,
              xq_w, xkv_w, xo_w, xq_norm, xk_norm,
              ffn_in_w, ffn_conv_w, ffn_conv_b, ffn_out_w,
              hyb_in_w, hyb_out_w, gdn_conv_w, gdn_a_log, gdn_dt_bias, gdn_norm,
              lru_conv_w, lru_conv_b, lru_wr, lru_br, lru_wi, lru_bi, lru_lambda,
              swa_in_w, swa_out_w, swa_q_norm, swa_k_norm, swa_sinks):
    cos, sin = rope_tables(positions, SWA_HD)
    for l in range(DEPTH):
        h = rmsnorm(x, norm_mix[l])
        if l % 2 == 0:
            e = l // 2
            qkv, z, b_raw, a_raw, lx, lg = split_cols(h @ hyb_in_w[e], HYB_SIZES)
            oa = gdn_mixer(qkv, z, b_raw, a_raw, gdn_conv_w[e], gdn_a_log[e], gdn_dt_bias[e], gdn_norm[e])
            ob = rglru_mixer(lx, lg, lru_conv_w[e], lru_conv_b[e], lru_wr[e], lru_br[e],
                             lru_wi[e], lru_bi[e], lru_lambda[e])
            mix = jnp.concatenate([oa, ob], axis=-1) @ hyb_out_w[e]
        else:
            o = l // 2
            mix = swa_mixer(h @ swa_in_w[o], swa_q_norm[o], swa_k_norm[o], swa_sinks[o], cos, sin) @ swa_out_w[o]
        x = x + mix
        x = x + cross_attn(rmsnorm(x, norm_cross[l]), rmsnorm(mem, norm_mem[l]),
                           xq_w[l], xkv_w[l], xo_w[l], xq_norm[l], xk_norm[l])
        x = x + conv_ffn(rmsnorm(x, norm_ffn[l]), ffn_in_w[l], ffn_conv_w[l], ffn_conv_b[l], ffn_out_w[l])
    return x
```

```python
import functools
import math

import numpy as np
import jax
import jax.numpy as jnp
from jax import lax
from jax.experimental import pallas as pl
from jax.experimental.pallas import tpu as pltpu

F32 = jnp.float32
BF16 = jnp.bfloat16

D_MODEL = 2048
DEPTH = 4
MEM_LEN = 256
EPS = 1e-6
GDN_HEADS = 8
GDN_DK = 128
GDN_DV = 128
GDN_CONV = 4
GDN_CHUNK = 64
GDN_QK = GDN_HEADS * GDN_DK
GDN_V = GDN_HEADS * GDN_DV
GDN_QKV = 2 * GDN_QK + GDN_V
LRU_WIDTH = D_MODEL // 2
LRU_BLOCKS = 8
LRU_BLOCK = LRU_WIDTH // LRU_BLOCKS
LRU_CONV = 4
LRU_C = 8.0
SWA_HEADS = 32
SWA_KV_HEADS = 8
SWA_GROUP = SWA_HEADS // SWA_KV_HEADS
SWA_HD = 64
SWA_HALF = SWA_HD // 2
SWA_WINDOW = 128
SWA_BLOCK = 128
SWA_Q = SWA_HEADS * SWA_HD
SWA_KV = SWA_KV_HEADS * SWA_HD
ROPE_THETA = 10000.0
X_HEADS = 4
X_HD = 128
X_INNER = X_HEADS * X_HD
D_FF = 5632
FFN_CONV = 3

V7X_LANES = 128
V7X_SUBLANES = 8
V7X_BF16_ROWS = 16
V7X_VMEM_BYTES = 64 * 1024 * 1024


def _vmem_limit(nbytes):
    return int(min(max(2 * nbytes, 16 * 1024 * 1024), V7X_VMEM_BYTES - 8 * 1024 * 1024))


def _params(semantics, nbytes):
    return pltpu.CompilerParams(dimension_semantics=semantics, vmem_limit_bytes=_vmem_limit(nbytes))


def _rms(x, g):
    return x * lax.rsqrt(jnp.mean(x * x, axis=-1, keepdims=True) + EPS) * g


def _sigmoid(x):
    return 1.0 / (1.0 + jnp.exp(-x))


def _softplus(x):
    return jnp.maximum(x, 0.0) + jnp.log1p(jnp.exp(-jnp.abs(x)))


def _dot(a, b):
    return jnp.dot(a, b, preferred_element_type=F32)


def _dot_nt(a, b):
    return lax.dot_general(a, b, (((1,), (1,)), ((), ())), preferred_element_type=F32)


def _dot_tn(a, b):
    return lax.dot_general(a, b, (((0,), (0,)), ((), ())), preferred_element_type=F32)


def _norm_matmul_kernel(x_ref, g_ref, w_ref, o_ref, h_ref):
    @pl.when(pl.program_id(1) == 0)
    def _():
        h_ref[...] = _rms(x_ref[...], g_ref[...]).astype(BF16)

    o_ref[...] = _dot(h_ref[...], w_ref[...]).astype(o_ref.dtype)


def _norm_matmul2_kernel(x_ref, g_ref, w_ref, w2_ref, o_ref, o2_ref, h_ref):
    @pl.when(pl.program_id(1) == 0)
    def _():
        hb = _rms(x_ref[...], g_ref[...]).astype(BF16)
        h_ref[...] = hb
        o2_ref[...] = _dot(hb, w2_ref[...])

    o_ref[...] = _dot(h_ref[...], w_ref[...]).astype(o_ref.dtype)


def _norm_matmul(x, g, w, w2=None, *, tm=1024, tn=1024, name):
    s, d = x.shape
    n = w.shape[1]
    tm, tn = min(tm, s), min(tn, n)
    grid = (s // tm, n // tn)
    nbytes = 2 * tm * d * 4 + tm * d * 2 + 2 * d * tn * 2 + 2 * tm * tn * 4
    x_spec = pl.BlockSpec((tm, d), lambda i, j: (i, 0))
    g_spec = pl.BlockSpec((1, d), lambda i, j: (0, 0))
    w_spec = pl.BlockSpec((d, tn), lambda i, j: (0, j))
    o_spec = pl.BlockSpec((tm, tn), lambda i, j: (i, j))
    scratch = [pltpu.VMEM((tm, d), BF16)]
    if w2 is None:
        return pl.pallas_call(
            _norm_matmul_kernel, grid=grid, name=name,
            in_specs=[x_spec, g_spec, w_spec], out_specs=o_spec,
            out_shape=jax.ShapeDtypeStruct((s, n), F32), scratch_shapes=scratch,
            compiler_params=_params(("parallel", "arbitrary"), nbytes),
        )(x, g, w)
    n2 = w2.shape[1]
    return pl.pallas_call(
        _norm_matmul2_kernel, grid=grid, name=name,
        in_specs=[x_spec, g_spec, w_spec, pl.BlockSpec((d, n2), lambda i, j: (0, 0))],
        out_specs=[o_spec, pl.BlockSpec((tm, n2), lambda i, j: (i, 0))],
        out_shape=[jax.ShapeDtypeStruct((s, n), F32), jax.ShapeDtypeStruct((s, n2), F32)],
        scratch_shapes=scratch,
        compiler_params=_params(("parallel", "arbitrary"), nbytes),
    )(x, g, w, w2)


def _matmul_res_kernel(a_ref, w_ref, x_ref, o_ref):
    o_ref[...] = x_ref[...] + _dot(a_ref[...], w_ref[...])


def _matmul_res(a, w, x, *, tm=1024, tn=1024, name):
    s, k = a.shape
    n = w.shape[1]
    tm, tn = min(tm, s), min(tn, n)
    nbytes = 2 * tm * k * 2 + 2 * k * tn * 2 + 4 * tm * tn * 4
    return pl.pallas_call(
        _matmul_res_kernel, grid=(s // tm, n // tn), name=name,
        in_specs=[pl.BlockSpec((tm, k), lambda i, j: (i, 0)),
                  pl.BlockSpec((k, tn), lambda i, j: (0, j)),
                  pl.BlockSpec((tm, tn), lambda i, j: (i, j))],
        out_specs=pl.BlockSpec((tm, tn), lambda i, j: (i, j)),
        out_shape=jax.ShapeDtypeStruct((s, n), F32),
        compiler_params=_params(("parallel", "parallel"), nbytes),
    )(a, w, x)


def _mem_kv_kernel(mem_ref, g_ref, w_ref, kg_ref, k_ref, v_ref):
    mn = _rms(mem_ref[...], g_ref[0]).astype(BF16)
    kv = _dot(mn, w_ref[0])
    kg = kg_ref[0]
    for h in range(X_HEADS):
        kh = kv[:, h * X_HD:(h + 1) * X_HD]
        k_ref[0, :, h * X_HD:(h + 1) * X_HD] = _rms(kh, kg).astype(BF16)
    v_ref[0] = kv[:, X_INNER:].astype(BF16)


def _mem_kv(mem, norm_mem, xkv_w, xk_norm):
    m, d = mem.shape
    nl = norm_mem.shape[0]
    out = jax.ShapeDtypeStruct((nl, m, X_INNER), BF16)
    o_spec = pl.BlockSpec((1, m, X_INNER), lambda l: (l, 0, 0))
    return pl.pallas_call(
        _mem_kv_kernel, grid=(nl,), name="mem_kv",
        in_specs=[pl.BlockSpec((m, d), lambda l: (0, 0)),
                  pl.BlockSpec((1, 1, d), lambda l: (l, 0, 0)),
                  pl.BlockSpec((1, d, 2 * X_INNER), lambda l: (l, 0, 0)),
                  pl.BlockSpec((1, 1, X_HD), lambda l: (l, 0, 0))],
        out_specs=[o_spec, o_spec], out_shape=[out, out],
        compiler_params=_params(("parallel",), 2 * d * 2 * X_INNER * 2 + m * d * 4),
    )(mem, norm_mem.reshape(nl, 1, d), xkv_w, xk_norm.reshape(nl, 1, X_HD))


def _cross_kernel(x_ref, g_ref, wq_ref, qg_ref, k_ref, v_ref, wo_ref, o_ref):
    x = x_ref[...]
    h = _rms(x, g_ref[...]).astype(BF16)
    q = _dot(h, wq_ref[...])
    qg = qg_ref[...]
    outs = []
    for hh in range(X_HEADS):
        sl = slice(hh * X_HD, (hh + 1) * X_HD)
        qn = _rms(q[:, sl], qg).astype(BF16)
        sc = _dot_nt(qn, k_ref[:, sl]) * (X_HD ** -0.5)
        p = jnp.exp(sc - jnp.max(sc, axis=-1, keepdims=True))
        inv = 1.0 / jnp.sum(p, axis=-1, keepdims=True)
        outs.append((_dot(p.astype(BF16), v_ref[:, sl]) * inv).astype(BF16))
    o = jnp.concatenate(outs, axis=-1)
    o_ref[...] = x + _dot(o, wo_ref[...])


def _cross_block(x, g, wq, qg, kn, v, wo, *, tm=512, name):
    s, d = x.shape
    tm = min(tm, s)
    m = kn.shape[0]
    nbytes = 4 * tm * d * 4 + 4 * d * X_INNER * 2 + 4 * m * X_INNER * 2 + 4 * tm * m * 4
    full = lambda shape: pl.BlockSpec(shape, lambda i: (0, 0))
    return pl.pallas_call(
        _cross_kernel, grid=(s // tm,), name=name,
        in_specs=[pl.BlockSpec((tm, d), lambda i: (i, 0)), full((1, d)), full((d, X_INNER)),
                  full((1, X_HD)), full((m, X_INNER)), full((m, X_INNER)), full((X_INNER, d))],
        out_specs=pl.BlockSpec((tm, d), lambda i: (i, 0)),
        out_shape=jax.ShapeDtypeStruct((s, d), F32),
        compiler_params=_params(("parallel",), nbytes),
    )(x, g, wq, qg, kn, v, wo)


FFN_HALO = V7X_BF16_ROWS


def _ffn_kernel(xh_ref, x_ref, g_ref, wg_ref, wu_ref, cw_ref, cb_ref, wo_ref, o_ref, h_ref):
    tm = x_ref.shape[0]

    @pl.when(pl.program_id(1) == 0)
    def _():
        g = g_ref[...]
        keep = (pl.program_id(0) > 0).astype(F32)
        h_ref[0:FFN_HALO, :] = _rms(xh_ref[...] * keep, g).astype(BF16)
        x = x_ref[...]
        h_ref[FFN_HALO:, :] = _rms(x, g).astype(BF16)
        o_ref[...] = x

    gt = _dot(h_ref[...], wg_ref[...])
    up = _dot(h_ref[FFN_HALO:, :], wu_ref[...])
    cw = cw_ref[...]
    y = cb_ref[...] + cw[FFN_CONV - 1:FFN_CONV] * gt[FFN_HALO:, :]
    for j in range(FFN_CONV - 1):
        back = FFN_CONV - 1 - j
        y = y + cw[j:j + 1] * gt[FFN_HALO - back:FFN_HALO - back + tm, :]
    act = (y * _sigmoid(y) * up).astype(BF16)
    o_ref[...] += _dot(act, wo_ref[...])


def _ffn_block(x, g, w_in, conv_w, conv_b, w_out, *, tm=512, tf=512, name):
    s, d = x.shape
    tm = min(tm, s)
    nf = D_FF // tf
    nbytes = 4 * tm * d * 4 + (tm + FFN_HALO) * d * 2 + 6 * d * tf * 2 + 4 * (tm + FFN_HALO) * tf * 4
    halo_blocks = tm // FFN_HALO
    return pl.pallas_call(
        _ffn_kernel, grid=(s // tm, nf), name=name,
        in_specs=[pl.BlockSpec((FFN_HALO, d), lambda i, j: (jnp.maximum(i * halo_blocks - 1, 0), 0)),
                  pl.BlockSpec((tm, d), lambda i, j: (i, 0)),
                  pl.BlockSpec((1, d), lambda i, j: (0, 0)),
                  pl.BlockSpec((d, tf), lambda i, j: (0, j)),
                  pl.BlockSpec((d, tf), lambda i, j: (0, j + nf)),
                  pl.BlockSpec((FFN_CONV, tf), lambda i, j: (0, j)),
                  pl.BlockSpec((1, tf), lambda i, j: (0, j)),
                  pl.BlockSpec((tf, d), lambda i, j: (j, 0))],
        out_specs=pl.BlockSpec((tm, d), lambda i, j: (i, 0)),
        out_shape=jax.ShapeDtypeStruct((s, d), F32),
        scratch_shapes=[pltpu.VMEM((tm + FFN_HALO, d), BF16)],
        compiler_params=_params(("parallel", "arbitrary"), nbytes),
    )(x, x, g, w_in, w_in, conv_w, conv_b, w_out)


GDN_TAIL = V7X_SUBLANES
NEUMANN_DOUBLINGS = int(math.log2(GDN_CHUNK)) - 1


def _gdn_kernel(qkv_ref, z_ref, ba_ref, cw_ref, alog_ref, dt_ref, ng_ref, o_ref, ext_ref, st_ref):
    c = GDN_CHUNK
    first = pl.program_id(0) == 0

    @pl.when(first)
    def _():
        ext_ref[0:GDN_TAIL, :] = jnp.zeros((GDN_TAIL, GDN_QKV), F32)
        st_ref[...] = jnp.zeros_like(st_ref)

    @pl.when(jnp.logical_not(first))
    def _():
        ext_ref[0:GDN_TAIL, :] = ext_ref[c:c + GDN_TAIL, :]

    ext_ref[GDN_TAIL:, :] = qkv_ref[...]

    def conv_silu(col):
        sl = slice(col * V7X_LANES, (col + 1) * V7X_LANES)
        cw = cw_ref[:, sl]
        y = cw[GDN_CONV - 1:GDN_CONV] * ext_ref[GDN_TAIL:GDN_TAIL + c, sl]
        for j in range(GDN_CONV - 1):
            back = GDN_CONV - 1 - j
            y = y + cw[j:j + 1] * ext_ref[GDN_TAIL - back:GDN_TAIL - back + c, sl]
        return y * _sigmoid(y)

    def l2n(t):
        return t * lax.rsqrt(jnp.sum(t * t, axis=-1, keepdims=True) + EPS)

    ba = ba_ref[...]
    lane = lax.broadcasted_iota(jnp.int32, ba.shape, 1)
    row = lax.broadcasted_iota(jnp.int32, ba.shape, 0)
    bg = jnp.where(lane < GDN_HEADS, _sigmoid(ba), -jnp.exp(alog_ref[...]) * _softplus(ba + dt_ref[...]))
    gcs = bg
    shift = 1
    while shift < c:
        gcs = gcs + jnp.where(row >= shift, pltpu.roll(gcs, shift, 0), 0.0)
        shift *= 2
    gct = jnp.concatenate([gcs, jnp.zeros_like(gcs)], axis=0).T

    ri = lax.broadcasted_iota(jnp.int32, (c, c), 0)
    ci = lax.broadcasted_iota(jnp.int32, (c, c), 1)
    causal = ri >= ci
    strict = ri > ci
    eye = (ri == ci).astype(F32)
    ng = ng_ref[...]

    for h in range(GDN_HEADS):
        q = l2n(conv_silu(h)) * (GDN_DK ** -0.5)
        k = l2n(conv_silu(GDN_HEADS + h))
        v = conv_silu(2 * GDN_HEADS + h)
        beta = bg[:, h:h + 1]
        gc_col = gcs[:, GDN_HEADS + h:GDN_HEADS + h + 1]
        gc_row = gct[GDN_HEADS + h:GDN_HEADS + h + 1, 0:c]
        g_last = gc_row[:, c - 1:c]
        decay = jnp.where(causal, jnp.exp(jnp.minimum(gc_col - gc_row, 0.0)), 0.0)
        egc = jnp.exp(gc_col)
        kb = k * beta
        kbf = k.astype(BF16)
        qk = _dot_nt(jnp.concatenate([q, kb], axis=0).astype(BF16), kbf)
        attn = qk[0:c] * decay
        lmat = jnp.where(strict, qk[c:2 * c] * decay, 0.0)
        pm = eye - lmat
        mpow = lmat.astype(BF16)
        for _ in range(NEUMANN_DOUBLINGS):
            msq = _dot(mpow, mpow)
            mpow = msq.astype(BF16)
            pm = pm + _dot(pm.astype(BF16), mpow)
        rhs = jnp.concatenate([v * beta, kb * egc], axis=1).astype(BF16)
        uw = _dot(pm.astype(BF16), rhs)
        u, w = uw[:, 0:GDN_DV], uw[:, GDN_DV:]
        state = st_ref[h]
        ws = _dot(jnp.concatenate([w, q * egc], axis=0).astype(BF16), state.astype(BF16))
        v_new = u - ws[0:c]
        v_new_b = v_new.astype(BF16)
        o = ws[c:2 * c] + _dot(attn.astype(BF16), v_new_b)
        k_dec = (k * jnp.exp(g_last - gc_col)).astype(BF16)
        st_ref[h] = state * jnp.exp(g_last) + _dot_tn(k_dec, v_new_b)
        zz = z_ref[:, h * GDN_DV:(h + 1) * GDN_DV]
        o_ref[:, h * GDN_DV:(h + 1) * GDN_DV] = (_rms(o, ng) * (zz * _sigmoid(zz))).astype(o_ref.dtype)


def _gdn_mixer(p, ba, conv_w, alog_vec, dt_vec, norm_g, *, name):
    s = p.shape[0]
    c = GDN_CHUNK
    full = lambda shape: pl.BlockSpec(shape, lambda i: (0, 0))
    return pl.pallas_call(
        _gdn_kernel, grid=(s // c,), name=name,
        in_specs=[pl.BlockSpec((c, GDN_QKV), lambda i: (i, 0)),
                  pl.BlockSpec((c, GDN_V), lambda i: (i, GDN_QKV // GDN_V)),
                  pl.BlockSpec((c, V7X_LANES), lambda i: (i, 0)),
                  full((GDN_CONV, GDN_QKV)), full((1, V7X_LANES)), full((1, V7X_LANES)), full((1, GDN_DV))],
        out_specs=pl.BlockSpec((c, GDN_V), lambda i: (i, 0)),
        out_shape=jax.ShapeDtypeStruct((s, GDN_V), BF16),
        scratch_shapes=[pltpu.VMEM((c + GDN_TAIL, GDN_QKV), F32),
                        pltpu.VMEM((GDN_HEADS, GDN_DK, GDN_DV), F32)],
        compiler_params=_params(("arbitrary",), 8 * c * GDN_QKV * 4),
    )(p, p, ba, conv_w, alog_vec, dt_vec, norm_g)


LRU_TAIL = V7X_SUBLANES


def _lru_kernel(x_ref, gate_ref, cw_ref, cb_ref, wr_ref, br_ref, wi_ref, bi_ref, lam_ref,
                o_ref, ext_ref, a_ref, b_ref, carry_ref):
    tt = x_ref.shape[0]
    first = pl.program_id(0) == 0

    @pl.when(first)
    def _():
        ext_ref[0:LRU_TAIL, :] = jnp.zeros((LRU_TAIL, LRU_WIDTH), F32)
        carry_ref[...] = jnp.zeros_like(carry_ref)

    @pl.when(jnp.logical_not(first))
    def _():
        ext_ref[0:LRU_TAIL, :] = ext_ref[tt:tt + LRU_TAIL, :]

    ext_ref[LRU_TAIL:, :] = x_ref[...]

    for blk in range(LRU_BLOCKS):
        sl = slice(blk * LRU_BLOCK, (blk + 1) * LRU_BLOCK)
        cw = cw_ref[:, sl]
        xc = cb_ref[:, sl] + cw[LRU_CONV - 1:LRU_CONV] * ext_ref[LRU_TAIL:LRU_TAIL + tt, sl]
        for j in range(LRU_CONV - 1):
            back = LRU_CONV - 1 - j
            xc = xc + cw[j:j + 1] * ext_ref[LRU_TAIL - back:LRU_TAIL - back + tt, sl]
        xb = xc.astype(BF16)
        r = _sigmoid(_dot(xb, wr_ref[blk]) + br_ref[:, sl])
        i = _sigmoid(_dot(xb, wi_ref[blk]) + bi_ref[:, sl])
        log_a = (-LRU_C) * r * _softplus(-lam_ref[:, sl])
        a = jnp.exp(log_a)
        one_minus_a2 = -jnp.tanh(log_a) * (1.0 + a * a)
        a_ref[:, sl] = a
        b_ref[:, sl] = jnp.sqrt(one_minus_a2) * (i * xc)

    row = lax.broadcasted_iota(jnp.int32, (V7X_SUBLANES, LRU_WIDTH), 0)

    def group(gi, hprev):
        r0 = pl.multiple_of(gi * V7X_SUBLANES, V7X_SUBLANES)
        a = a_ref[pl.ds(r0, V7X_SUBLANES), :]
        b = b_ref[pl.ds(r0, V7X_SUBLANES), :]
        shift = 1
        while shift < V7X_SUBLANES:
            keep = row >= shift
            b = b + a * jnp.where(keep, pltpu.roll(b, shift, 0), 0.0)
            a = a * jnp.where(keep, pltpu.roll(a, shift, 0), 1.0)
            shift *= 2
        hcur = b + a * hprev
        b_ref[pl.ds(r0, V7X_SUBLANES), :] = hcur
        return jnp.broadcast_to(hcur[V7X_SUBLANES - 1:V7X_SUBLANES, :], hcur.shape)

    carry_ref[...] = lax.fori_loop(0, tt // V7X_SUBLANES, group, carry_ref[...])

    gate = gate_ref[...]
    gelu = 0.5 * gate * (1.0 + jnp.tanh(math.sqrt(2.0 / math.pi) * (gate + 0.044715 * (gate * gate * gate))))
    o_ref[...] = (b_ref[...] * gelu).astype(o_ref.dtype)


def _lru_mixer(p, conv_w, conv_b, wr, br, wi, bi, lam, *, col0, tt=256, name):
    s = p.shape[0]
    tt = min(tt, s)
    cb = col0 // LRU_WIDTH
    full2 = lambda shape: pl.BlockSpec(shape, lambda i: (0, 0))
    full3 = lambda shape: pl.BlockSpec(shape, lambda i: (0, 0, 0))
    return pl.pallas_call(
        _lru_kernel, grid=(s // tt,), name=name,
        in_specs=[pl.BlockSpec((tt, LRU_WIDTH), lambda i: (i, cb)),
                  pl.BlockSpec((tt, LRU_WIDTH), lambda i: (i, cb + 1)),
                  full2((LRU_CONV, LRU_WIDTH)), full2((1, LRU_WIDTH)),
                  full3((LRU_BLOCKS, LRU_BLOCK, LRU_BLOCK)), full2((1, LRU_WIDTH)),
                  full3((LRU_BLOCKS, LRU_BLOCK, LRU_BLOCK)), full2((1, LRU_WIDTH)),
                  full2((1, LRU_WIDTH))],
        out_specs=pl.BlockSpec((tt, LRU_WIDTH), lambda i: (i, 0)),
        out_shape=jax.ShapeDtypeStruct((s, LRU_WIDTH), BF16),
        scratch_shapes=[pltpu.VMEM((tt + LRU_TAIL, LRU_WIDTH), F32),
                        pltpu.VMEM((tt, LRU_WIDTH), F32),
                        pltpu.VMEM((tt, LRU_WIDTH), F32),
                        pltpu.VMEM((V7X_SUBLANES, LRU_WIDTH), F32)],
        compiler_params=_params(("arbitrary",), 10 * tt * LRU_WIDTH * 4),
    )(p, p, conv_w, conv_b, wr, br, wi, bi, lam)


def _rope_kernel(pos_ref, inv_ref, cos_ref, sin_ref):
    ang = pos_ref[...].astype(F32) * inv_ref[...]
    cos_ref[...] = jnp.cos(ang)
    sin_ref[...] = jnp.sin(ang)


def _rope_tables(pos_col, inv_lanes, *, tm=1024):
    s = pos_col.shape[0]
    tm = min(tm, s)
    out = jax.ShapeDtypeStruct((s, V7X_LANES), F32)
    o_spec = pl.BlockSpec((tm, V7X_LANES), lambda i: (i, 0))
    return pl.pallas_call(
        _rope_kernel, grid=(s // tm,), name="rope_tables",
        in_specs=[pl.BlockSpec((tm, 1), lambda i: (i, 0)), pl.BlockSpec((1, V7X_LANES), lambda i: (0, 0))],
        out_specs=[o_spec, o_spec], out_shape=[out, out],
        compiler_params=_params(("parallel",), 8 * tm * V7X_LANES * 4),
    )(pos_col, inv_lanes)


SWA_PAIR = 2 * V7X_LANES
SWA_Q_PAIRS = SWA_Q // SWA_PAIR
SWA_KV_PAIRS = SWA_KV // SWA_PAIR
SWA_GROUPS_PER_PAIR = V7X_LANES // SWA_HALF


def _swa_kernel(sink_ref, qkv_ref, cos_ref, sin_ref, qg_ref, kg_ref, seg_ref, o_ref, k_ref, v_ref):
    blk = SWA_BLOCK
    first = pl.program_id(0) == 0

    @pl.when(first)
    def _():
        k_ref[0:blk, :] = jnp.zeros((blk, SWA_KV), BF16)
        v_ref[0:blk, :] = jnp.zeros((blk, SWA_KV), BF16)

    @pl.when(jnp.logical_not(first))
    def _():
        k_ref[0:blk, :] = k_ref[blk:2 * blk, :]
        v_ref[0:blk, :] = v_ref[blk:2 * blk, :]

    cos = cos_ref[...]
    sin = sin_ref[...]
    seg = seg_ref[...]

    def norm_rope(col0, gain_ref):
        a = qkv_ref[:, col0:col0 + V7X_LANES]
        b = qkv_ref[:, col0 + V7X_LANES:col0 + SWA_PAIR]
        sq = a * a + b * b
        hi = sq.astype(BF16)
        lo = (sq - hi.astype(F32)).astype(BF16)
        ss = _dot(hi, seg) + _dot(lo, seg)
        scale = lax.rsqrt(ss * (1.0 / SWA_HD) + EPS)
        a = a * scale * gain_ref[:, 0:V7X_LANES]
        b = b * scale * gain_ref[:, V7X_LANES:SWA_PAIR]
        return a * cos - b * sin, b * cos + a * sin

    for u in range(SWA_KV_PAIRS):
        ka, kb = norm_rope(SWA_Q + u * SWA_PAIR, kg_ref)
        k_ref[blk:2 * blk, u * SWA_PAIR:u * SWA_PAIR + V7X_LANES] = ka.astype(BF16)
        k_ref[blk:2 * blk, u * SWA_PAIR + V7X_LANES:(u + 1) * SWA_PAIR] = kb.astype(BF16)
    v_ref[blk:2 * blk, :] = qkv_ref[:, SWA_Q + SWA_KV:SWA_Q + 2 * SWA_KV].astype(BF16)

    qi = lax.broadcasted_iota(jnp.int32, (blk, 2 * blk), 0)
    kj = lax.broadcasted_iota(jnp.int32, (blk, 2 * blk), 1)
    lo_key = jnp.where(first, blk, 0)
    valid = (kj > qi) & (kj <= qi + SWA_WINDOW) & (kj >= lo_key)
    lane_group = lax.broadcasted_iota(jnp.int32, (blk, SWA_PAIR), 1) % V7X_LANES // SWA_HALF

    for t in range(SWA_Q_PAIRS):
        u = t // SWA_GROUPS_PER_PAIR
        qa, qb = norm_rope(t * SWA_PAIR, qg_ref)
        qp = jnp.concatenate([qa, qb], axis=1)
        qs = jnp.concatenate([jnp.where(lane_group == gl, qp, 0.0) for gl in range(SWA_GROUPS_PER_PAIR)],
                             axis=0).astype(BF16)
        kp = k_ref[:, u * SWA_PAIR:(u + 1) * SWA_PAIR]
        sc_all = _dot_nt(qs, kp) * (SWA_HD ** -0.5)
        ps, invs = [], []
        for gl in range(SWA_GROUPS_PER_PAIR):
            head = SWA_GROUP * (SWA_GROUPS_PER_PAIR * u + gl) + t % SWA_GROUPS_PER_PAIR
            sink = sink_ref[head]
            sc = jnp.where(valid, sc_all[gl * blk:(gl + 1) * blk], -jnp.inf)
            m = jnp.maximum(jnp.max(sc, axis=-1, keepdims=True), sink)
            p = jnp.exp(sc - m)
            invs.append(1.0 / (jnp.sum(p, axis=-1, keepdims=True) + jnp.exp(sink - m)))
            ps.append(p.astype(BF16))
        pv = _dot(jnp.concatenate(ps, axis=0), v_ref[:, u * SWA_PAIR:(u + 1) * SWA_PAIR])
        out = jnp.zeros((blk, SWA_PAIR), F32)
        for gl in range(SWA_GROUPS_PER_PAIR):
            out = jnp.where(lane_group == gl, pv[gl * blk:(gl + 1) * blk] * invs[gl], out)
        o_ref[:, t * SWA_PAIR:(t + 1) * SWA_PAIR] = out.astype(o_ref.dtype)


def _swa_mixer(qkv, sinks, cos, sin, qg, kg, seg, *, name):
    s, n = qkv.shape
    blk = SWA_BLOCK
    full = lambda shape: pl.BlockSpec(shape, lambda i: (0, 0))
    return pl.pallas_call(
        _swa_kernel, grid=(s // blk,), name=name,
        in_specs=[pl.BlockSpec(memory_space=pltpu.SMEM),
                  pl.BlockSpec((blk, n), lambda i: (i, 0)),
                  pl.BlockSpec((blk, V7X_LANES), lambda i: (i, 0)),
                  pl.BlockSpec((blk, V7X_LANES), lambda i: (i, 0)),
                  full((1, SWA_PAIR)), full((1, SWA_PAIR)), full((V7X_LANES, V7X_LANES))],
        out_specs=pl.BlockSpec((blk, SWA_Q), lambda i: (i, 0)),
        out_shape=jax.ShapeDtypeStruct((s, SWA_Q), BF16),
        scratch_shapes=[pltpu.VMEM((2 * blk, SWA_KV), BF16), pltpu.VMEM((2 * blk, SWA_KV), BF16)],
        compiler_params=_params(("arbitrary",), 8 * blk * n * 4),
    )(sinks, qkv, cos, sin, qg, kg, seg)


def _swa_layout():
    q_cols = np.zeros((SWA_Q,), np.int32)
    for t in range(SWA_Q_PAIRS):
        for half in range(2):
            for gl in range(SWA_GROUPS_PER_PAIR):
                kv_head = SWA_GROUPS_PER_PAIR * (t // SWA_GROUPS_PER_PAIR) + gl
                head = SWA_GROUP * kv_head + t % SWA_GROUPS_PER_PAIR
                new = t * SWA_PAIR + half * V7X_LANES + gl * SWA_HALF
                q_cols[new:new + SWA_HALF] = head * SWA_HD + half * SWA_HALF + np.arange(SWA_HALF)
    kv_cols = np.zeros((SWA_KV,), np.int32)
    for u in range(SWA_KV_PAIRS):
        for half in range(2):
            for gl in range(SWA_GROUPS_PER_PAIR):
                kv_head = SWA_GROUPS_PER_PAIR * u + gl
                new = u * SWA_PAIR + half * V7X_LANES + gl * SWA_HALF
                kv_cols[new:new + SWA_HALF] = kv_head * SWA_HD + half * SWA_HALF + np.arange(SWA_HALF)
    gain_cols = np.concatenate([np.tile(np.arange(SWA_HALF), SWA_GROUPS_PER_PAIR),
                                np.tile(SWA_HALF + np.arange(SWA_HALF), SWA_GROUPS_PER_PAIR)]).astype(np.int32)
    lanes = np.arange(V7X_LANES)
    seg = (lanes[:, None] // SWA_HALF == lanes[None, :] // SWA_HALF).astype(np.float32)
    return q_cols, kv_cols, gain_cols, seg


def _lane_vec(vals, offset):
    return jnp.zeros((1, V7X_LANES), F32).at[0, offset:offset + vals.shape[0]].set(vals.astype(F32))


def kernel(x, mem, positions, norm_mix, norm_cross, norm_mem, norm_ffn, xq_w, xkv_w, xo_w, xq_norm, xk_norm, ffn_in_w, ffn_conv_w, ffn_conv_b, ffn_out_w, hyb_in_w, hyb_out_w, gdn_conv_w, gdn_a_log, gdn_dt_bias, gdn_norm, lru_conv_w, lru_conv_b, lru_wr, lru_br, lru_wi, lru_bi, lru_lambda, swa_in_w, swa_out_w, swa_q_norm, swa_k_norm, swa_sinks):
    bsz, s, d = x.shape
    assert bsz == 1 and d == D_MODEL and s % 1024 == 0
    xs = x[0]
    row = lambda v: v.reshape(1, -1).astype(F32)

    q_cols, kv_cols, gain_cols, seg = _swa_layout()
    in_cols = np.concatenate([q_cols, SWA_Q + kv_cols, SWA_Q + SWA_KV + kv_cols])
    inv = 1.0 / (ROPE_THETA ** (jnp.arange(0, SWA_HD, 2, dtype=F32) / SWA_HD))
    inv_lanes = jnp.tile(inv, SWA_GROUPS_PER_PAIR).reshape(1, V7X_LANES)
    cos, sin = _rope_tables(positions.reshape(s, 1), inv_lanes)
    seg_b = jnp.asarray(seg, BF16)

    kn_all, v_all = _mem_kv(mem[0], norm_mem, xkv_w.astype(BF16), xk_norm)

    ba0 = GDN_QKV + GDN_V
    lru0 = ba0 + 2 * GDN_HEADS

    for l in range(DEPTH):
        if l % 2 == 0:
            e = l // 2
            w = hyb_in_w[e]
            w_main = jnp.concatenate([w[:, :ba0], w[:, lru0:]], axis=1).astype(BF16)
            w_ba = jnp.pad(w[:, ba0:lru0], ((0, 0), (0, V7X_LANES - 2 * GDN_HEADS))).astype(BF16)
            p, ba = _norm_matmul(xs, row(norm_mix[l]), w_main, w_ba, name=f"hyb_in_{l}")
            oa = _gdn_mixer(p, ba, gdn_conv_w[e], _lane_vec(gdn_a_log[e], GDN_HEADS),
                            _lane_vec(gdn_dt_bias[e], GDN_HEADS), row(gdn_norm[e]), name=f"gdn_{l}")
            ob = _lru_mixer(p, lru_conv_w[e], row(lru_conv_b[e]), lru_wr[e].astype(BF16), row(lru_br[e]),
                            lru_wi[e].astype(BF16), row(lru_bi[e]), row(lru_lambda[e]), col0=ba0, name=f"lru_{l}")
            mix = jnp.concatenate([oa, ob], axis=-1)
            xs = _matmul_res(mix, hyb_out_w[e].astype(BF16), xs, name=f"hyb_out_{l}")
        else:
            o = l // 2
            w_in = jnp.take(swa_in_w[o], in_cols, axis=1).astype(BF16)
            qkv = _norm_matmul(xs, row(norm_mix[l]), w_in, name=f"swa_in_{l}")
            att = _swa_mixer(qkv, swa_sinks[o].astype(F32), cos, sin,
                             row(swa_q_norm[o][gain_cols]), row(swa_k_norm[o][gain_cols]), seg_b, name=f"swa_{l}")
            w_out = jnp.take(swa_out_w[o], q_cols, axis=0).astype(BF16)
            xs = _matmul_res(att, w_out, xs, name=f"swa_out_{l}")
        xs = _cross_block(xs, row(norm_cross[l]), xq_w[l].astype(BF16), row(xq_norm[l]),
                          kn_all[l], v_all[l], xo_w[l].astype(BF16), name=f"cross_{l}")
        xs = _ffn_block(xs, row(norm_ffn[l]), ffn_in_w[l].astype(BF16), ffn_conv_w[l],
                        row(ffn_conv_b[l]), ffn_out_w[l].astype(BF16), name=f"ffn_{l}")
    return xs[None]
```

```python
import functools
import math

import numpy as np
import jax
import jax.numpy as jnp
from jax import lax
from jax.experimental import pallas as pl
from jax.experimental.pallas import tpu as pltpu

F32 = jnp.float32
BF16 = jnp.bfloat16

D_MODEL = 2048
DEPTH = 4
MEM_LEN = 256
EPS = 1e-6
GDN_HEADS = 8
GDN_DK = 128
GDN_DV = 128
GDN_CONV = 4
GDN_CHUNK = 64
GDN_QK = GDN_HEADS * GDN_DK
GDN_V = GDN_HEADS * GDN_DV
GDN_QKV = 2 * GDN_QK + GDN_V
LRU_WIDTH = D_MODEL // 2
LRU_BLOCKS = 8
LRU_BLOCK = LRU_WIDTH // LRU_BLOCKS
LRU_CONV = 4
LRU_C = 8.0
SWA_HEADS = 32
SWA_KV_HEADS = 8
SWA_GROUP = SWA_HEADS // SWA_KV_HEADS
SWA_HD = 64
SWA_HALF = SWA_HD // 2
SWA_WINDOW = 128
SWA_BLOCK = 128
SWA_Q = SWA_HEADS * SWA_HD
SWA_KV = SWA_KV_HEADS * SWA_HD
ROPE_THETA = 10000.0
X_HEADS = 4
X_HD = 128
X_INNER = X_HEADS * X_HD
D_FF = 5632
FFN_CONV = 3

V7X_LANES = 128
V7X_SUBLANES = 8
V7X_BF16_ROWS = 16
V7X_VMEM_BYTES = 64 * 1024 * 1024


def _vmem_limit(nbytes):
    return int(min(max(2 * nbytes, 16 * 1024 * 1024), V7X_VMEM_BYTES - 8 * 1024 * 1024))


def _params(semantics, nbytes):
    return pltpu.CompilerParams(dimension_semantics=semantics, vmem_limit_bytes=_vmem_limit(nbytes))


def _rms(x, g):
    return x * lax.rsqrt(jnp.mean(x * x, axis=-1, keepdims=True) + EPS) * g


def _sigmoid(x):
    return 1.0 / (1.0 + jnp.exp(-x))


def _softplus(x):
    return jnp.maximum(x, 0.0) + jnp.log1p(jnp.exp(-jnp.abs(x)))


def _dot(a, b):
    return jnp.dot(a, b, preferred_element_type=F32)


def _dot_nt(a, b):
    return lax.dot_general(a, b, (((1,), (1,)), ((), ())), preferred_element_type=F32)


def _dot_tn(a, b):
    return lax.dot_general(a, b, (((0,), (0,)), ((), ())), preferred_element_type=F32)


def _norm_matmul_kernel(x_ref, g_ref, w_ref, o_ref, h_ref):
    @pl.when(pl.program_id(1) == 0)
    def _():
        h_ref[...] = _rms(x_ref[...], g_ref[...]).astype(BF16)

    o_ref[...] = _dot(h_ref[...], w_ref[...]).astype(o_ref.dtype)


def _norm_matmul2_kernel(x_ref, g_ref, w_ref, w2_ref, o_ref, o2_ref, h_ref):
    @pl.when(pl.program_id(1) == 0)
    def _():
        hb = _rms(x_ref[...], g_ref[...]).astype(BF16)
        h_ref[...] = hb
        o2_ref[...] = _dot(hb, w2_ref[...])

    o_ref[...] = _dot(h_ref[...], w_ref[...]).astype(o_ref.dtype)


def _norm_matmul(x, g, w, w2=None, *, tm=1024, tn=1024, name):
    s, d = x.shape
    n = w.shape[1]
    tm, tn = min(tm, s), min(tn, n)
    grid = (s // tm, n // tn)
    nbytes = 2 * tm * d * 4 + tm * d * 2 + 2 * d * tn * 2 + 2 * tm * tn * 4
    x_spec = pl.BlockSpec((tm, d), lambda i, j: (i, 0))
    g_spec = pl.BlockSpec((1, d), lambda i, j: (0, 0))
    w_spec = pl.BlockSpec((d, tn), lambda i, j: (0, j))
    o_spec = pl.BlockSpec((tm, tn), lambda i, j: (i, j))
    scratch = [pltpu.VMEM((tm, d), BF16)]
    if w2 is None:
        return pl.pallas_call(
            _norm_matmul_kernel, grid=grid, name=name,
            in_specs=[x_spec, g_spec, w_spec], out_specs=o_spec,
            out_shape=jax.ShapeDtypeStruct((s, n), F32), scratch_shapes=scratch,
            compiler_params=_params(("parallel", "arbitrary"), nbytes),
        )(x, g, w)
    n2 = w2.shape[1]
    return pl.pallas_call(
        _norm_matmul2_kernel, grid=grid, name=name,
        in_specs=[x_spec, g_spec, w_spec, pl.BlockSpec((d, n2), lambda i, j: (0, 0))],
        out_specs=[o_spec, pl.BlockSpec((tm, n2), lambda i, j: (i, 0))],
        out_shape=[jax.ShapeDtypeStruct((s, n), F32), jax.ShapeDtypeStruct((s, n2), F32)],
        scratch_shapes=scratch,
        compiler_params=_params(("parallel", "arbitrary"), nbytes),
    )(x, g, w, w2)


def _matmul_res_kernel(a_ref, w_ref, x_ref, o_ref):
    o_ref[...] = x_ref[...] + _dot(a_ref[...], w_ref[...])


def _matmul_res(a, w, x, *, tm=1024, tn=1024, name):
    s, k = a.shape
    n = w.shape[1]
    tm, tn = min(tm, s), min(tn, n)
    nbytes = 2 * tm * k * 2 + 2 * k * tn * 2 + 4 * tm * tn * 4
    return pl.pallas_call(
        _matmul_res_kernel, grid=(s // tm, n // tn), name=name,
        in_specs=[pl.BlockSpec((tm, k), lambda i, j: (i, 0)),
                  pl.BlockSpec((k, tn), lambda i, j: (0, j)),
                  pl.BlockSpec((tm, tn), lambda i, j: (i, j))],
        out_specs=pl.BlockSpec((tm, tn), lambda i, j: (i, j)),
        out_shape=jax.ShapeDtypeStruct((s, n), F32),
        compiler_params=_params(("parallel", "parallel"), nbytes),
    )(a, w, x)


def _mem_kv_kernel(mem_ref, g_ref, w_ref, kg_ref, k_ref, v_ref):
    mn = _rms(mem_ref[...], g_ref[0]).astype(BF16)
    kv = _dot(mn, w_ref[0])
    kg = kg_ref[0]
    for h in range(X_HEADS):
        kh = kv[:, h * X_HD:(h + 1) * X_HD]
        k_ref[0, :, h * X_HD:(h + 1) * X_HD] = _rms(kh, kg).astype(BF16)
    v_ref[0] = kv[:, X_INNER:].astype(BF16)


def _mem_kv(mem, norm_mem, xkv_w, xk_norm):
    m, d = mem.shape
    nl = norm_mem.shape[0]
    out = jax.ShapeDtypeStruct((nl, m, X_INNER), BF16)
    o_spec = pl.BlockSpec((1, m, X_INNER), lambda l: (l, 0, 0))
    return pl.pallas_call(
        _mem_kv_kernel, grid=(nl,), name="mem_kv",
        in_specs=[pl.BlockSpec((m, d), lambda l: (0, 0)),
                  pl.BlockSpec((1, 1, d), lambda l: (l, 0, 0)),
                  pl.BlockSpec((1, d, 2 * X_INNER), lambda l: (l, 0, 0)),
                  pl.BlockSpec((1, 1, X_HD), lambda l: (l, 0, 0))],
        out_specs=[o_spec, o_spec], out_shape=[out, out],
        compiler_params=_params(("parallel",), 2 * d * 2 * X_INNER * 2 + m * d * 4),
    )(mem, norm_mem.reshape(nl, 1, d), xkv_w, xk_norm.reshape(nl, 1, X_HD))


def _cross_kernel(x_ref, g_ref, wq_ref, qg_ref, k_ref, v_ref, wo_ref, o_ref):
    x = x_ref[...]
    h = _rms(x, g_ref[...]).astype(BF16)
    q = _dot(h, wq_ref[...])
    qg = qg_ref[...]
    outs = []
    for hh in range(X_HEADS):
        sl = slice(hh * X_HD, (hh + 1) * X_HD)
        qn = _rms(q[:, sl], qg).astype(BF16)
        sc = _dot_nt(qn, k_ref[:, sl]) * (X_HD ** -0.5)
        p = jnp.exp(sc - jnp.max(sc, axis=-1, keepdims=True))
        inv = 1.0 / jnp.sum(p, axis=-1, keepdims=True)
        outs.append((_dot(p.astype(BF16), v_ref[:, sl]) * inv).astype(BF16))
    o = jnp.concatenate(outs, axis=-1)
    o_ref[...] = x + _dot(o, wo_ref[...])


def _cross_block(x, g, wq, qg, kn, v, wo, *, tm=512, name):
    s, d = x.shape
    tm = min(tm, s)
    m = kn.shape[0]
    nbytes = 4 * tm * d * 4 + 4 * d * X_INNER * 2 + 4 * m * X_INNER * 2 + 4 * tm * m * 4
    full = lambda shape: pl.BlockSpec(shape, lambda i: (0, 0))
    return pl.pallas_call(
        _cross_kernel, grid=(s // tm,), name=name,
        in_specs=[pl.BlockSpec((tm, d), lambda i: (i, 0)), full((1, d)), full((d, X_INNER)),
                  full((1, X_HD)), full((m, X_INNER)), full((m, X_INNER)), full((X_INNER, d))],
        out_specs=pl.BlockSpec((tm, d), lambda i: (i, 0)),
        out_shape=jax.ShapeDtypeStruct((s, d), F32),
        compiler_params=_params(("parallel",), nbytes),
    )(x, g, wq, qg, kn, v, wo)


FFN_HALO = V7X_BF16_ROWS


def _ffn_kernel(xh_ref, x_ref, g_ref, wg_ref, wu_ref, cw_ref, cb_ref, wo_ref, o_ref, h_ref):
    tm = x_ref.shape[0]

    @pl.when(pl.program_id(1) == 0)
    def _():
        g = g_ref[...]
        keep = (pl.program_id(0) > 0).astype(F32)
        h_ref[0:FFN_HALO, :] = _rms(xh_ref[...] * keep, g).astype(BF16)
        x = x_ref[...]
        h_ref[FFN_HALO:, :] = _rms(x, g).astype(BF16)
        o_ref[...] = x

    gt = _dot(h_ref[...], wg_ref[...])
    up = _dot(h_ref[FFN_HALO:, :], wu_ref[...])
    cw = cw_ref[...]
    y = cb_ref[...] + cw[FFN_CONV - 1:FFN_CONV] * gt[FFN_HALO:, :]
    for j in range(FFN_CONV - 1):
        back = FFN_CONV - 1 - j
        y = y + cw[j:j + 1] * gt[FFN_HALO - back:FFN_HALO - back + tm, :]
    act = (y * _sigmoid(y) * up).astype(BF16)
    o_ref[...] += _dot(act, wo_ref[...])


def _ffn_block(x, g, w_in, conv_w, conv_b, w_out, *, tm=512, tf=512, name):
    s, d = x.shape
    tm = min(tm, s)
    nf = D_FF // tf
    nbytes = 4 * tm * d * 4 + (tm + FFN_HALO) * d * 2 + 6 * d * tf * 2 + 4 * (tm + FFN_HALO) * tf * 4
    halo_blocks = tm // FFN_HALO
    return pl.pallas_call(
        _ffn_kernel, grid=(s // tm, nf), name=name,
        in_specs=[pl.BlockSpec((FFN_HALO, d), lambda i, j: (jnp.maximum(i * halo_blocks - 1, 0), 0)),
                  pl.BlockSpec((tm, d), lambda i, j: (i, 0)),
                  pl.BlockSpec((1, d), lambda i, j: (0, 0)),
                  pl.BlockSpec((d, tf), lambda i, j: (0, j)),
                  pl.BlockSpec((d, tf), lambda i, j: (0, j + nf)),
                  pl.BlockSpec((FFN_CONV, tf), lambda i, j: (0, j)),
                  pl.BlockSpec((1, tf), lambda i, j: (0, j)),
                  pl.BlockSpec((tf, d), lambda i, j: (j, 0))],
        out_specs=pl.BlockSpec((tm, d), lambda i, j: (i, 0)),
        out_shape=jax.ShapeDtypeStruct((s, d), F32),
        scratch_shapes=[pltpu.VMEM((tm + FFN_HALO, d), BF16)],
        compiler_params=_params(("parallel", "arbitrary"), nbytes),
    )(x, x, g, w_in, w_in, conv_w, conv_b, w_out)


GDN_TAIL = V7X_SUBLANES
NEUMANN_DOUBLINGS = int(math.log2(GDN_CHUNK)) - 1


def _gdn_kernel(qkv_ref, z_ref, ba_ref, cw_ref, alog_ref, dt_ref, ng_ref, o_ref, ext_ref, st_ref):
    c = GDN_CHUNK
    first = pl.program_id(0) == 0

    @pl.when(first)
    def _():
        ext_ref[0:GDN_TAIL, :] = jnp.zeros((GDN_TAIL, GDN_QKV), F32)
        st_ref[...] = jnp.zeros_like(st_ref)

    @pl.when(jnp.logical_not(first))
    def _():
        ext_ref[0:GDN_TAIL, :] = ext_ref[c:c + GDN_TAIL, :]

    ext_ref[GDN_TAIL:, :] = qkv_ref[...]

    def conv_silu(col):
        sl = slice(col * V7X_LANES, (col + 1) * V7X_LANES)
        cw = cw_ref[:, sl]
        y = cw[GDN_CONV - 1:GDN_CONV] * ext_ref[GDN_TAIL:GDN_TAIL + c, sl]
        for j in range(GDN_CONV - 1):
            back = GDN_CONV - 1 - j
            y = y + cw[j:j + 1] * ext_ref[GDN_TAIL - back:GDN_TAIL - back + c, sl]
        return y * _sigmoid(y)

    def l2n(t):
        return t * lax.rsqrt(jnp.sum(t * t, axis=-1, keepdims=True) + EPS)

    ba = ba_ref[...]
    lane = lax.broadcasted_iota(jnp.int32, ba.shape, 1)
    row = lax.broadcasted_iota(jnp.int32, ba.shape, 0)
    bg = jnp.where(lane < GDN_HEADS, _sigmoid(ba), -jnp.exp(alog_ref[...]) * _softplus(ba + dt_ref[...]))
    gcs = bg
    shift = 1
    while shift < c:
        gcs = gcs + jnp.where(row >= shift, pltpu.roll(gcs, shift, 0), 0.0)
        shift *= 2
    gct = jnp.concatenate([gcs, jnp.zeros_like(gcs)], axis=0).T

    ri = lax.broadcasted_iota(jnp.int32, (c, c), 0)
    ci = lax.broadcasted_iota(jnp.int32, (c, c), 1)
    causal = ri >= ci
    strict = ri > ci
    eye = (ri == ci).astype(F32)
    ng = ng_ref[...]

    heads = range(GDN_HEADS)
    q = [l2n(conv_silu(h)) * (GDN_DK ** -0.5) for h in heads]
    k = [l2n(conv_silu(GDN_HEADS + h)) for h in heads]
    v = [conv_silu(2 * GDN_HEADS + h) for h in heads]
    beta = [bg[:, h:h + 1] for h in heads]
    gc_col = [gcs[:, GDN_HEADS + h:GDN_HEADS + h + 1] for h in heads]
    gc_row = [gct[GDN_HEADS + h:GDN_HEADS + h + 1, 0:c] for h in heads]
    g_last = [gr[:, c - 1:c] for gr in gc_row]
    decay = [jnp.where(causal, jnp.exp(jnp.minimum(gc_col[h] - gc_row[h], 0.0)), 0.0) for h in heads]
    egc = [jnp.exp(gc_col[h]) for h in heads]
    kb = [k[h] * beta[h] for h in heads]
    qk = [_dot_nt(jnp.concatenate([q[h], kb[h]], axis=0).astype(BF16), k[h].astype(BF16)) for h in heads]
    attn = [(qk[h][0:c] * decay[h]).astype(BF16) for h in heads]
    lmat = [jnp.where(strict, qk[h][c:2 * c] * decay[h], 0.0) for h in heads]
    pm = [eye - lmat[h] for h in heads]
    mpow = [lmat[h].astype(BF16) for h in heads]
    for _ in range(NEUMANN_DOUBLINGS):
        mpow = [_dot(mpow[h], mpow[h]).astype(BF16) for h in heads]
        pm = [pm[h] + _dot(pm[h].astype(BF16), mpow[h]) for h in heads]
    uw = [_dot(pm[h].astype(BF16),
               jnp.concatenate([v[h] * beta[h], kb[h] * egc[h]], axis=1).astype(BF16)) for h in heads]
    state = [st_ref[h] for h in heads]
    ws = [_dot(jnp.concatenate([uw[h][:, GDN_DV:], q[h] * egc[h]], axis=0).astype(BF16),
               state[h].astype(BF16)) for h in heads]
    v_new = [(uw[h][:, 0:GDN_DV] - ws[h][0:c]).astype(BF16) for h in heads]
    k_dec = [(k[h] * jnp.exp(g_last[h] - gc_col[h])).astype(BF16) for h in heads]
    for h in heads:
        st_ref[h] = state[h] * jnp.exp(g_last[h]) + _dot_tn(k_dec[h], v_new[h])
    o = [ws[h][c:2 * c] + _dot(attn[h], v_new[h]) for h in heads]
    for h in heads:
        zz = z_ref[:, h * GDN_DV:(h + 1) * GDN_DV]
        o_ref[:, h * GDN_DV:(h + 1) * GDN_DV] = (_rms(o[h], ng) * (zz * _sigmoid(zz))).astype(o_ref.dtype)


def _gdn_mixer(p, ba, conv_w, alog_vec, dt_vec, norm_g, *, name):
    s = p.shape[0]
    c = GDN_CHUNK
    full = lambda shape: pl.BlockSpec(shape, lambda i: (0, 0))
    return pl.pallas_call(
        _gdn_kernel, grid=(s // c,), name=name,
        in_specs=[pl.BlockSpec((c, GDN_QKV), lambda i: (i, 0)),
                  pl.BlockSpec((c, GDN_V), lambda i: (i, GDN_QKV // GDN_V)),
                  pl.BlockSpec((c, V7X_LANES), lambda i: (i, 0)),
                  full((GDN_CONV, GDN_QKV)), full((1, V7X_LANES)), full((1, V7X_LANES)), full((1, GDN_DV))],
        out_specs=pl.BlockSpec((c, GDN_V), lambda i: (i, 0)),
        out_shape=jax.ShapeDtypeStruct((s, GDN_V), BF16),
        scratch_shapes=[pltpu.VMEM((c + GDN_TAIL, GDN_QKV), F32),
                        pltpu.VMEM((GDN_HEADS, GDN_DK, GDN_DV), F32)],
        compiler_params=_params(("arbitrary",), 8 * c * GDN_QKV * 4),
    )(p, p, ba, conv_w, alog_vec, dt_vec, norm_g)


LRU_TAIL = V7X_SUBLANES


def _lru_kernel(x_ref, gate_ref, cw_ref, cb_ref, wr_ref, br_ref, wi_ref, bi_ref, lam_ref,
                o_ref, ext_ref, a_ref, b_ref, carry_ref):
    tt = x_ref.shape[0]
    first = pl.program_id(0) == 0

    @pl.when(first)
    def _():
        ext_ref[0:LRU_TAIL, :] = jnp.zeros((LRU_TAIL, LRU_WIDTH), F32)
        carry_ref[...] = jnp.zeros_like(carry_ref)

    @pl.when(jnp.logical_not(first))
    def _():
        ext_ref[0:LRU_TAIL, :] = ext_ref[tt:tt + LRU_TAIL, :]

    ext_ref[LRU_TAIL:, :] = x_ref[...]

    for blk in range(LRU_BLOCKS):
        sl = slice(blk * LRU_BLOCK, (blk + 1) * LRU_BLOCK)
        cw = cw_ref[:, sl]
        xc = cb_ref[:, sl] + cw[LRU_CONV - 1:LRU_CONV] * ext_ref[LRU_TAIL:LRU_TAIL + tt, sl]
        for j in range(LRU_CONV - 1):
            back = LRU_CONV - 1 - j
            xc = xc + cw[j:j + 1] * ext_ref[LRU_TAIL - back:LRU_TAIL - back + tt, sl]
        xb = xc.astype(BF16)
        r = _sigmoid(_dot(xb, wr_ref[blk]) + br_ref[:, sl])
        i = _sigmoid(_dot(xb, wi_ref[blk]) + bi_ref[:, sl])
        log_a = (-LRU_C) * r * _softplus(-lam_ref[:, sl])
        a = jnp.exp(log_a)
        one_minus_a2 = -jnp.tanh(log_a) * (1.0 + a * a)
        a_ref[:, sl] = a
        b_ref[:, sl] = jnp.sqrt(one_minus_a2) * (i * xc)

    row = lax.broadcasted_iota(jnp.int32, (V7X_SUBLANES, LRU_WIDTH), 0)

    def group(gi, hprev):
        r0 = pl.multiple_of(gi * V7X_SUBLANES, V7X_SUBLANES)
        a = a_ref[pl.ds(r0, V7X_SUBLANES), :]
        b = b_ref[pl.ds(r0, V7X_SUBLANES), :]
        shift = 1
        while shift < V7X_SUBLANES:
            keep = row >= shift
            b = b + a * jnp.where(keep, pltpu.roll(b, shift, 0), 0.0)
            a = a * jnp.where(keep, pltpu.roll(a, shift, 0), 1.0)
            shift *= 2
        hcur = b + a * hprev
        b_ref[pl.ds(r0, V7X_SUBLANES), :] = hcur
        return jnp.broadcast_to(hcur[V7X_SUBLANES - 1:V7X_SUBLANES, :], hcur.shape)

    carry_ref[...] = lax.fori_loop(0, tt // V7X_SUBLANES, group, carry_ref[...])

    gate = gate_ref[...]
    gelu = 0.5 * gate * (1.0 + jnp.tanh(math.sqrt(2.0 / math.pi) * (gate + 0.044715 * (gate * gate * gate))))
    o_ref[...] = (b_ref[...] * gelu).astype(o_ref.dtype)


def _lru_mixer(p, conv_w, conv_b, wr, br, wi, bi, lam, *, col0, tt=256, name):
    s = p.shape[0]
    tt = min(tt, s)
    cb = col0 // LRU_WIDTH
    full2 = lambda shape: pl.BlockSpec(shape, lambda i: (0, 0))
    full3 = lambda shape: pl.BlockSpec(shape, lambda i: (0, 0, 0))
    return pl.pallas_call(
        _lru_kernel, grid=(s // tt,), name=name,
        in_specs=[pl.BlockSpec((tt, LRU_WIDTH), lambda i: (i, cb)),
                  pl.BlockSpec((tt, LRU_WIDTH), lambda i: (i, cb + 1)),
                  full2((LRU_CONV, LRU_WIDTH)), full2((1, LRU_WIDTH)),
                  full3((LRU_BLOCKS, LRU_BLOCK, LRU_BLOCK)), full2((1, LRU_WIDTH)),
                  full3((LRU_BLOCKS, LRU_BLOCK, LRU_BLOCK)), full2((1, LRU_WIDTH)),
                  full2((1, LRU_WIDTH))],
        out_specs=pl.BlockSpec((tt, LRU_WIDTH), lambda i: (i, 0)),
        out_shape=jax.ShapeDtypeStruct((s, LRU_WIDTH), BF16),
        scratch_shapes=[pltpu.VMEM((tt + LRU_TAIL, LRU_WIDTH), F32),
                        pltpu.VMEM((tt, LRU_WIDTH), F32),
                        pltpu.VMEM((tt, LRU_WIDTH), F32),
                        pltpu.VMEM((V7X_SUBLANES, LRU_WIDTH), F32)],
        compiler_params=_params(("arbitrary",), 10 * tt * LRU_WIDTH * 4),
    )(p, p, conv_w, conv_b, wr, br, wi, bi, lam)


def _rope_kernel(pos_ref, inv_ref, cos_ref, sin_ref):
    ang = pos_ref[...].astype(F32) * inv_ref[...]
    cos_ref[...] = jnp.cos(ang)
    sin_ref[...] = jnp.sin(ang)


def _rope_tables(pos_col, inv_lanes, *, tm=1024):
    s = pos_col.shape[0]
    tm = min(tm, s)
    out = jax.ShapeDtypeStruct((s, V7X_LANES), F32)
    o_spec = pl.BlockSpec((tm, V7X_LANES), lambda i: (i, 0))
    return pl.pallas_call(
        _rope_kernel, grid=(s // tm,), name="rope_tables",
        in_specs=[pl.BlockSpec((tm, 1), lambda i: (i, 0)), pl.BlockSpec((1, V7X_LANES), lambda i: (0, 0))],
        out_specs=[o_spec, o_spec], out_shape=[out, out],
        compiler_params=_params(("parallel",), 8 * tm * V7X_LANES * 4),
    )(pos_col, inv_lanes)


SWA_PAIR = 2 * V7X_LANES
SWA_Q_PAIRS = SWA_Q // SWA_PAIR
SWA_KV_PAIRS = SWA_KV // SWA_PAIR
SWA_GROUPS_PER_PAIR = V7X_LANES // SWA_HALF


def _swa_kernel(sink_ref, qkv_ref, cos_ref, sin_ref, qg_ref, kg_ref, seg_ref, o_ref, k_ref, v_ref):
    blk = SWA_BLOCK
    first = pl.program_id(0) == 0

    @pl.when(first)
    def _():
        k_ref[0:blk, :] = jnp.zeros((blk, SWA_KV), BF16)
        v_ref[0:blk, :] = jnp.zeros((blk, SWA_KV), BF16)

    @pl.when(jnp.logical_not(first))
    def _():
        k_ref[0:blk, :] = k_ref[blk:2 * blk, :]
        v_ref[0:blk, :] = v_ref[blk:2 * blk, :]

    cos = cos_ref[...]
    sin = sin_ref[...]
    seg = seg_ref[...]

    def norm_rope(col0, gain_ref):
        a = qkv_ref[:, col0:col0 + V7X_LANES]
        b = qkv_ref[:, col0 + V7X_LANES:col0 + SWA_PAIR]
        sq = a * a + b * b
        hi = sq.astype(BF16)
        lo = (sq - hi.astype(F32)).astype(BF16)
        ss = _dot(hi, seg) + _dot(lo, seg)
        scale = lax.rsqrt(ss * (1.0 / SWA_HD) + EPS)
        a = a * scale * gain_ref[:, 0:V7X_LANES]
        b = b * scale * gain_ref[:, V7X_LANES:SWA_PAIR]
        return a * cos - b * sin, b * cos + a * sin

    for u in range(SWA_KV_PAIRS):
        ka, kb = norm_rope(SWA_Q + u * SWA_PAIR, kg_ref)
        k_ref[blk:2 * blk, u * SWA_PAIR:u * SWA_PAIR + V7X_LANES] = ka.astype(BF16)
        k_ref[blk:2 * blk, u * SWA_PAIR + V7X_LANES:(u + 1) * SWA_PAIR] = kb.astype(BF16)
    v_ref[blk:2 * blk, :] = qkv_ref[:, SWA_Q + SWA_KV:SWA_Q + 2 * SWA_KV].astype(BF16)

    qi = lax.broadcasted_iota(jnp.int32, (blk, 2 * blk), 0)
    kj = lax.broadcasted_iota(jnp.int32, (blk, 2 * blk), 1)
    lo_key = jnp.where(first, blk, 0)
    valid = (kj > qi) & (kj <= qi + SWA_WINDOW) & (kj >= lo_key)
    lane_group = lax.broadcasted_iota(jnp.int32, (blk, SWA_PAIR), 1) % V7X_LANES // SWA_HALF

    for t in range(SWA_Q_PAIRS):
        u = t // SWA_GROUPS_PER_PAIR
        qa, qb = norm_rope(t * SWA_PAIR, qg_ref)
        qp = jnp.concatenate([qa, qb], axis=1)
        qs = jnp.concatenate([jnp.where(lane_group == gl, qp, 0.0) for gl in range(SWA_GROUPS_PER_PAIR)],
                             axis=0).astype(BF16)
        kp = k_ref[:, u * SWA_PAIR:(u + 1) * SWA_PAIR]
        sc_all = _dot_nt(qs, kp) * (SWA_HD ** -0.5)
        ps, invs = [], []
        for gl in range(SWA_GROUPS_PER_PAIR):
            head = SWA_GROUP * (SWA_GROUPS_PER_PAIR * u + gl) + t % SWA_GROUPS_PER_PAIR
            sink = sink_ref[head]
            sc = jnp.where(valid, sc_all[gl * blk:(gl + 1) * blk], -jnp.inf)
            m = jnp.maximum(jnp.max(sc, axis=-1, keepdims=True), sink)
            p = jnp.exp(sc - m)
            invs.append(1.0 / (jnp.sum(p, axis=-1, keepdims=True) + jnp.exp(sink - m)))
            ps.append(p.astype(BF16))
        pv = _dot(jnp.concatenate(ps, axis=0), v_ref[:, u * SWA_PAIR:(u + 1) * SWA_PAIR])
        out = jnp.zeros((blk, SWA_PAIR), F32)
        for gl in range(SWA_GROUPS_PER_PAIR):
            out = jnp.where(lane_group == gl, pv[gl * blk:(gl + 1) * blk] * invs[gl], out)
        o_ref[:, t * SWA_PAIR:(t + 1) * SWA_PAIR] = out.astype(o_ref.dtype)


def _swa_mixer(qkv, sinks, cos, sin, qg, kg, seg, *, name):
    s, n = qkv.shape
    blk = SWA_BLOCK
    full = lambda shape: pl.BlockSpec(shape, lambda i: (0, 0))
    return pl.pallas_call(
        _swa_kernel, grid=(s // blk,), name=name,
        in_specs=[pl.BlockSpec(memory_space=pltpu.SMEM),
                  pl.BlockSpec((blk, n), lambda i: (i, 0)),
                  pl.BlockSpec((blk, V7X_LANES), lambda i: (i, 0)),
                  pl.BlockSpec((blk, V7X_LANES), lambda i: (i, 0)),
                  full((1, SWA_PAIR)), full((1, SWA_PAIR)), full((V7X_LANES, V7X_LANES))],
        out_specs=pl.BlockSpec((blk, SWA_Q), lambda i: (i, 0)),
        out_shape=jax.ShapeDtypeStruct((s, SWA_Q), BF16),
        scratch_shapes=[pltpu.VMEM((2 * blk, SWA_KV), BF16), pltpu.VMEM((2 * blk, SWA_KV), BF16)],
        compiler_params=_params(("arbitrary",), 8 * blk * n * 4),
    )(sinks, qkv, cos, sin, qg, kg, seg)


def _swa_layout():
    q_cols = np.zeros((SWA_Q,), np.int32)
    for t in range(SWA_Q_PAIRS):
        for half in range(2):
            for gl in range(SWA_GROUPS_PER_PAIR):
                kv_head = SWA_GROUPS_PER_PAIR * (t // SWA_GROUPS_PER_PAIR) + gl
                head = SWA_GROUP * kv_head + t % SWA_GROUPS_PER_PAIR
                new = t * SWA_PAIR + half * V7X_LANES + gl * SWA_HALF
                q_cols[new:new + SWA_HALF] = head * SWA_HD + half * SWA_HALF + np.arange(SWA_HALF)
    kv_cols = np.zeros((SWA_KV,), np.int32)
    for u in range(SWA_KV_PAIRS):
        for half in range(2):
            for gl in range(SWA_GROUPS_PER_PAIR):
                kv_head = SWA_GROUPS_PER_PAIR * u + gl
                new = u * SWA_PAIR + half * V7X_LANES + gl * SWA_HALF
                kv_cols[new:new + SWA_HALF] = kv_head * SWA_HD + half * SWA_HALF + np.arange(SWA_HALF)
    gain_cols = np.concatenate([np.tile(np.arange(SWA_HALF), SWA_GROUPS_PER_PAIR),
                                np.tile(SWA_HALF + np.arange(SWA_HALF), SWA_GROUPS_PER_PAIR)]).astype(np.int32)
    lanes = np.arange(V7X_LANES)
    seg = (lanes[:, None] // SWA_HALF == lanes[None, :] // SWA_HALF).astype(np.float32)
    return q_cols, kv_cols, gain_cols, seg


def _lane_vec(vals, offset):
    return jnp.zeros((1, V7X_LANES), F32).at[0, offset:offset + vals.shape[0]].set(vals.astype(F32))


def kernel(x, mem, positions, norm_mix, norm_cross, norm_mem, norm_ffn, xq_w, xkv_w, xo_w, xq_norm, xk_norm, ffn_in_w, ffn_conv_w, ffn_conv_b, ffn_out_w, hyb_in_w, hyb_out_w, gdn_conv_w, gdn_a_log, gdn_dt_bias, gdn_norm, lru_conv_w, lru_conv_b, lru_wr, lru_br, lru_wi, lru_bi, lru_lambda, swa_in_w, swa_out_w, swa_q_norm, swa_k_norm, swa_sinks):
    bsz, s, d = x.shape
    assert bsz == 1 and d == D_MODEL and s % 1024 == 0
    xs = x[0]
    row = lambda v: v.reshape(1, -1).astype(F32)

    q_cols, kv_cols, gain_cols, seg = _swa_layout()
    in_cols = np.concatenate([q_cols, SWA_Q + kv_cols, SWA_Q + SWA_KV + kv_cols])
    inv = 1.0 / (ROPE_THETA ** (jnp.arange(0, SWA_HD, 2, dtype=F32) / SWA_HD))
    inv_lanes = jnp.tile(inv, SWA_GROUPS_PER_PAIR).reshape(1, V7X_LANES)
    cos, sin = _rope_tables(positions.reshape(s, 1), inv_lanes)
    seg_b = jnp.asarray(seg, BF16)

    kn_all, v_all = _mem_kv(mem[0], norm_mem, xkv_w.astype(BF16), xk_norm)

    ba0 = GDN_QKV + GDN_V
    lru0 = ba0 + 2 * GDN_HEADS

    for l in range(DEPTH):
        if l % 2 == 0:
            e = l // 2
            w = hyb_in_w[e]
            w_main = jnp.concatenate([w[:, :ba0], w[:, lru0:]], axis=1).astype(BF16)
            w_ba = jnp.pad(w[:, ba0:lru0], ((0, 0), (0, V7X_LANES - 2 * GDN_HEADS))).astype(BF16)
            p, ba = _norm_matmul(xs, row(norm_mix[l]), w_main, w_ba, name=f"hyb_in_{l}")
            oa = _gdn_mixer(p, ba, gdn_conv_w[e], _lane_vec(gdn_a_log[e], GDN_HEADS),
                            _lane_vec(gdn_dt_bias[e], GDN_HEADS), row(gdn_norm[e]), name=f"gdn_{l}")
            ob = _lru_mixer(p, lru_conv_w[e], row(lru_conv_b[e]), lru_wr[e].astype(BF16), row(lru_br[e]),
                            lru_wi[e].astype(BF16), row(lru_bi[e]), row(lru_lambda[e]), col0=ba0, name=f"lru_{l}")
            mix = jnp.concatenate([oa, ob], axis=-1)
            xs = _matmul_res(mix, hyb_out_w[e].astype(BF16), xs, name=f"hyb_out_{l}")
        else:
            o = l // 2
            w_in = jnp.take(swa_in_w[o], in_cols, axis=1).astype(BF16)
            qkv = _norm_matmul(xs, row(norm_mix[l]), w_in, name=f"swa_in_{l}")
            att = _swa_mixer(qkv, swa_sinks[o].astype(F32), cos, sin,
                             row(swa_q_norm[o][gain_cols]), row(swa_k_norm[o][gain_cols]), seg_b, name=f"swa_{l}")
            w_out = jnp.take(swa_out_w[o], q_cols, axis=0).astype(BF16)
            xs = _matmul_res(att, w_out, xs, name=f"swa_out_{l}")
        xs = _cross_block(xs, row(norm_cross[l]), xq_w[l].astype(BF16), row(xq_norm[l]),
                          kn_all[l], v_all[l], xo_w[l].astype(BF16), name=f"cross_{l}")
        xs = _ffn_block(xs, row(norm_ffn[l]), ffn_in_w[l].astype(BF16), ffn_conv_w[l],
                        row(ffn_conv_b[l]), ffn_out_w[l].astype(BF16), name=f"ffn_{l}")
    return xs[None]
```

```python
import math

import jax
import jax.numpy as jnp
from jax import lax
from jax.experimental import pallas as pl
from jax.experimental.pallas import tpu as pltpu

F32 = jnp.float32
BF16 = jnp.bfloat16

D_MODEL = 2048
DEPTH = 4
MEM_LEN = 256
EPS = 1e-6
GDN_HEADS = 8
GDN_DK = 128
GDN_DV = 128
GDN_CONV = 4
GDN_CHUNK = 64
GDN_QK = GDN_HEADS * GDN_DK
GDN_V = GDN_HEADS * GDN_DV
GDN_QKV = 2 * GDN_QK + GDN_V
LRU_WIDTH = D_MODEL // 2
LRU_BLOCKS = 8
LRU_BLOCK = LRU_WIDTH // LRU_BLOCKS
LRU_CONV = 4
LRU_C = 8.0
SWA_HEADS = 32
SWA_KV_HEADS = 8
SWA_GROUP = SWA_HEADS // SWA_KV_HEADS
SWA_HD = 64
SWA_HALF = SWA_HD // 2
SWA_WINDOW = 128
SWA_BLOCK = 128
SWA_Q = SWA_HEADS * SWA_HD
SWA_KV = SWA_KV_HEADS * SWA_HD
ROPE_THETA = 10000.0
X_HEADS = 4
X_HD = 128
X_INNER = X_HEADS * X_HD
D_FF = 5632
FFN_CONV = 3

V7X_LANES = 128
V7X_SUBLANES = 8
V7X_BF16_ROWS = 16
V7X_VMEM_BYTES = 64 * 1024 * 1024

HALO = V7X_BF16_ROWS


def _vmem_limit(nbytes):
    return int(min(max(2 * nbytes, 16 * 1024 * 1024), V7X_VMEM_BYTES - 8 * 1024 * 1024))


def _params(semantics, nbytes):
    return pltpu.CompilerParams(dimension_semantics=semantics, vmem_limit_bytes=_vmem_limit(nbytes))


def _rms(x, g):
    return x * lax.rsqrt(jnp.mean(x * x, axis=-1, keepdims=True) + EPS) * g


def _sigmoid(x):
    return 1.0 / (1.0 + jnp.exp(-x))


def _softplus(x):
    return jnp.maximum(x, 0.0) + jnp.log1p(jnp.exp(-jnp.abs(x)))


def _gelu_tanh(x):
    return 0.5 * x * (1.0 + jnp.tanh(math.sqrt(2.0 / math.pi) * (x + 0.044715 * (x * x * x))))


def _dot(a, b):
    return jnp.dot(a, b, preferred_element_type=F32)


def _dot_nt(a, b):
    return lax.dot_general(a, b, (((1,), (1,)), ((), ())), preferred_element_type=F32)


def _dot_tn(a, b):
    return lax.dot_general(a, b, (((0,), (0,)), ((), ())), preferred_element_type=F32)


def _halo_index(tm):
    per_tile = tm // HALO
    return lambda i, j: (jnp.maximum(i * per_tile - 1, 0), 0)


def _causal_conv(cw_ref, src_ref, row0, rows, width):
    y = cw_ref[width - 1:width, :] * src_ref[row0:row0 + rows, :]
    for j in range(width - 1):
        back = width - 1 - j
        y = y + cw_ref[j:j + 1, :] * src_ref[row0 - back:row0 - back + rows, :]
    return y


def _norm_matmul_kernel(x_ref, g_ref, w_ref, o_ref, h_ref):
    @pl.when(pl.program_id(1) == 0)
    def _():
        h_ref[...] = _rms(x_ref[...], g_ref[...]).astype(BF16)

    o_ref[...] = _dot(h_ref[...], w_ref[...]).astype(o_ref.dtype)


def _norm_matmul(x, g_all, w_all, l, wl, *, tm=1024, tn=1024, name):
    s, d = x.shape
    n = w_all.shape[-1]
    tm, tn = min(tm, s), min(tn, n)
    nbytes = 2 * tm * d * 4 + tm * d * 2 + 2 * d * tn * 2 + 2 * tm * tn * 4
    return pl.pallas_call(
        _norm_matmul_kernel, grid=(s // tm, n // tn), name=name,
        in_specs=[pl.BlockSpec((tm, d), lambda i, j: (i, 0)),
                  pl.BlockSpec((None, 1, d), lambda i, j: (l, 0, 0)),
                  pl.BlockSpec((None, d, tn), lambda i, j: (wl, 0, j))],
        out_specs=pl.BlockSpec((tm, tn), lambda i, j: (i, j)),
        out_shape=jax.ShapeDtypeStruct((s, n), F32),
        scratch_shapes=[pltpu.VMEM((tm, d), BF16)],
        compiler_params=_params(("parallel", "arbitrary"), nbytes),
    )(x, g_all, w_all)


HYB_TN = 1024
HYB_BLOCKS = 6
HYB_EPI_ROWS = 256


HYB_PIECE = 2 * V7X_LANES


def _hyb_in_kernel(xh_ref, x_ref, g_ref, wg_ref, wl_ref, wba_ref, gcw_ref, lcw_ref, lcb_ref,
                   p_ref, ba_ref, h_ref, *pe_refs):
    tm = x_ref.shape[0]
    j = pl.program_id(1)
    tiles = range(0, tm, HYB_EPI_ROWS)
    npieces = len(pe_refs)

    @pl.when(j == 0)
    def _():
        g = g_ref[...]
        keep = (pl.program_id(0) > 0).astype(F32)
        h_ref[0:HALO, :] = _rms(xh_ref[...] * keep, g).astype(BF16)
        hb = _rms(x_ref[...], g).astype(BF16)
        h_ref[HALO:, :] = hb
        ba_ref[...] = _dot(hb, wba_ref[...])

    def matmul(blk, piece):
        w_ref = wg_ref if blk < 4 else wl_ref
        pe_refs[piece][...] = _dot(h_ref[...], w_ref[:, piece * HYB_PIECE:(piece + 1) * HYB_PIECE])

    def epilogue(blk, piece):
        src = pe_refs[piece]
        cols = slice(piece * HYB_PIECE, (piece + 1) * HYB_PIECE)
        for r0 in tiles:
            rows = slice(r0, r0 + HYB_EPI_ROWS)
            if blk <= 2:
                y = _causal_conv(gcw_ref.at[:, cols], src, HALO + r0, HYB_EPI_ROWS, GDN_CONV)
                y = y * _sigmoid(y)
                if blk == 2:
                    p_ref[rows, cols] = y.astype(p_ref.dtype)
                    continue
                qscale = GDN_DK ** -0.5 if blk == 0 else 1.0
                for h in range(HYB_PIECE // GDN_DK):
                    yh = y[:, h * GDN_DK:(h + 1) * GDN_DK]
                    inv = lax.rsqrt(jnp.sum(yh * yh, axis=-1, keepdims=True) + EPS) * qscale
                    c0 = piece * HYB_PIECE + h * GDN_DK
                    p_ref[rows, c0:c0 + GDN_DK] = (yh * inv).astype(p_ref.dtype)
            elif blk == 3:
                z = src[HALO + r0:HALO + r0 + HYB_EPI_ROWS, :]
                p_ref[rows, cols] = (z * _sigmoid(z)).astype(p_ref.dtype)
            elif blk == 4:
                xc = _causal_conv(lcw_ref.at[:, cols], src, HALO + r0, HYB_EPI_ROWS, LRU_CONV) + lcb_ref[:, cols]
                p_ref[rows, cols] = xc.astype(p_ref.dtype)
            else:
                gate = src[HALO + r0:HALO + r0 + HYB_EPI_ROWS, :]
                p_ref[rows, cols] = _gelu_tanh(gate).astype(p_ref.dtype)

    for blk in range(HYB_BLOCKS):
        @pl.when(j == blk)
        def _(blk=blk):
            matmul(blk, 0)
            for piece in range(npieces):
                if piece + 1 < npieces:
                    matmul(blk, piece + 1)
                epilogue(blk, piece)


def _hyb_in(x, g_all, w_gdn, w_lru, w_ba, gdn_cw, lru_cw, lru_cb, l, e, *, tm=512, name):
    s, d = x.shape
    tm, tn = min(tm, s), HYB_TN
    nbytes = (2 * tm * d * 4 + (tm + HALO) * d * 2 + 4 * d * tn * 2 + (tm + HALO) * tn * 4
              + 2 * tm * tn * 2 + 8 * HYB_EPI_ROWS * HYB_PIECE * 4)
    return pl.pallas_call(
        _hyb_in_kernel, grid=(s // tm, HYB_BLOCKS), name=name,
        in_specs=[pl.BlockSpec((HALO, d), _halo_index(tm)),
                  pl.BlockSpec((tm, d), lambda i, j: (i, 0)),
                  pl.BlockSpec((None, 1, d), lambda i, j: (l, 0, 0)),
                  pl.BlockSpec((None, d, tn), lambda i, j: (e, 0, jnp.minimum(j, 3))),
                  pl.BlockSpec((None, d, tn), lambda i, j: (e, 0, jnp.maximum(j - 4, 0))),
                  pl.BlockSpec((None, d, V7X_LANES), lambda i, j: (e, 0, 0)),
                  pl.BlockSpec((None, GDN_CONV, tn), lambda i, j: (e, 0, jnp.minimum(j, 2))),
                  pl.BlockSpec((None, LRU_CONV, LRU_WIDTH), lambda i, j: (e, 0, 0)),
                  pl.BlockSpec((None, 1, LRU_WIDTH), lambda i, j: (e, 0, 0))],
        out_specs=[pl.BlockSpec((tm, tn), lambda i, j: (i, j)),
                   pl.BlockSpec((tm, V7X_LANES), lambda i, j: (i, 0))],
        out_shape=[jax.ShapeDtypeStruct((s, HYB_BLOCKS * tn), BF16),
                   jax.ShapeDtypeStruct((s, V7X_LANES), F32)],
        scratch_shapes=[pltpu.VMEM((tm + HALO, d), BF16)]
        + [pltpu.VMEM((tm + HALO, HYB_PIECE), F32) for _ in range(tn // HYB_PIECE)],
        compiler_params=_params(("parallel", "arbitrary"), nbytes),
    )(x, x, g_all, w_gdn, w_lru, w_ba, gdn_cw, lru_cw, lru_cb)


def _matmul_res_kernel(a_ref, w_ref, x_ref, o_ref):
    o_ref[...] = x_ref[...] + _dot(a_ref[...], w_ref[...])


def _matmul_res2_kernel(a1_ref, a2_ref, w_ref, x_ref, o_ref):
    k1 = a1_ref.shape[1]
    o_ref[...] = x_ref[...] + _dot(a1_ref[...], w_ref[0:k1, :]) + _dot(a2_ref[...], w_ref[k1:, :])


def _matmul_res(a_list, w_all, wl, x, *, tm=1024, tn=1024, name):
    s, n = x.shape
    k = w_all.shape[1]
    tm, tn = min(tm, s), min(tn, n)
    nbytes = 2 * tm * k * 2 + 2 * k * tn * 2 + 4 * tm * tn * 4
    body = _matmul_res_kernel if len(a_list) == 1 else _matmul_res2_kernel
    return pl.pallas_call(
        body, grid=(s // tm, n // tn), name=name,
        in_specs=[pl.BlockSpec((tm, a.shape[1]), lambda i, j: (i, 0)) for a in a_list]
        + [pl.BlockSpec((None, k, tn), lambda i, j: (wl, 0, j)),
           pl.BlockSpec((tm, tn), lambda i, j: (i, j))],
        out_specs=pl.BlockSpec((tm, tn), lambda i, j: (i, j)),
        out_shape=jax.ShapeDtypeStruct((s, n), F32),
        compiler_params=_params(("parallel", "parallel"), nbytes),
    )(*a_list, w_all, x)


def _mem_kv_kernel(mem_ref, g_ref, w_ref, kg_ref, k_ref, v_ref):
    mn = _rms(mem_ref[...], g_ref[...]).astype(BF16)
    kv = _dot(mn, w_ref[...])
    kg = kg_ref[...]
    for h in range(X_HEADS):
        kh = kv[:, h * X_HD:(h + 1) * X_HD]
        k_ref[:, h * X_HD:(h + 1) * X_HD] = _rms(kh, kg).astype(BF16)
    v_ref[...] = kv[:, X_INNER:].astype(BF16)


def _mem_kv(mem, norm_mem, xkv_w, xk_norm):
    m, d = mem.shape
    nl = norm_mem.shape[0]
    out = jax.ShapeDtypeStruct((nl, m, X_INNER), BF16)
    o_spec = pl.BlockSpec((None, m, X_INNER), lambda l: (l, 0, 0))
    return pl.pallas_call(
        _mem_kv_kernel, grid=(nl,), name="mem_kv",
        in_specs=[pl.BlockSpec((m, d), lambda l: (0, 0)),
                  pl.BlockSpec((None, 1, d), lambda l: (l, 0, 0)),
                  pl.BlockSpec((None, d, 2 * X_INNER), lambda l: (l, 0, 0)),
                  pl.BlockSpec((None, 1, X_HD), lambda l: (l, 0, 0))],
        out_specs=[o_spec, o_spec], out_shape=[out, out],
        compiler_params=_params(("parallel",), 2 * d * 2 * X_INNER * 2 + m * d * 4),
    )(mem, norm_mem, xkv_w, xk_norm)


def _cross_kernel(x_ref, g_ref, wq_ref, qg_ref, k_ref, v_ref, wo_ref, o_ref):
    x = x_ref[...]
    h = _rms(x, g_ref[...]).astype(BF16)
    q = _dot(h, wq_ref[...])
    qg = qg_ref[...]
    outs = []
    for hh in range(X_HEADS):
        sl = slice(hh * X_HD, (hh + 1) * X_HD)
        qn = _rms(q[:, sl], qg).astype(BF16)
        sc = _dot_nt(qn, k_ref[:, sl]) * (X_HD ** -0.5)
        p = jnp.exp(sc - jnp.max(sc, axis=-1, keepdims=True))
        inv = 1.0 / jnp.sum(p, axis=-1, keepdims=True)
        outs.append((_dot(p.astype(BF16), v_ref[:, sl]) * inv).astype(BF16))
    o = jnp.concatenate(outs, axis=-1)
    o_ref[...] = x + _dot(o, wo_ref[...])


def _cross_block(x, g_all, wq_all, qg_all, kn_all, v_all, wo_all, l, *, tm=512, name):
    s, d = x.shape
    tm = min(tm, s)
    m = kn_all.shape[1]
    nbytes = 4 * tm * d * 4 + 4 * d * X_INNER * 2 + 4 * m * X_INNER * 2 + 4 * tm * m * 4
    layer = lambda shape: pl.BlockSpec((None,) + shape, lambda i: (l, 0, 0))
    return pl.pallas_call(
        _cross_kernel, grid=(s // tm,), name=name,
        in_specs=[pl.BlockSpec((tm, d), lambda i: (i, 0)), layer((1, d)), layer((d, X_INNER)),
                  layer((1, X_HD)), layer((m, X_INNER)), layer((m, X_INNER)), layer((X_INNER, d))],
        out_specs=pl.BlockSpec((tm, d), lambda i: (i, 0)),
        out_shape=jax.ShapeDtypeStruct((s, d), F32),
        compiler_params=_params(("parallel",), nbytes),
    )(x, g_all, wq_all, qg_all, kn_all, v_all, wo_all)


def _ffn_kernel(xh_ref, x_ref, g_ref, wg_ref, wu_ref, cw_ref, cb_ref, wo_ref, o_ref, h_ref, gt_ref):
    tm = x_ref.shape[0]

    @pl.when(pl.program_id(1) == 0)
    def _():
        g = g_ref[...]
        keep = (pl.program_id(0) > 0).astype(F32)
        h_ref[0:HALO, :] = _rms(xh_ref[...] * keep, g).astype(BF16)
        x = x_ref[...]
        h_ref[HALO:, :] = _rms(x, g).astype(BF16)
        o_ref[...] = x

    gt_ref[...] = _dot(h_ref[...], wg_ref[...])
    up = _dot(h_ref[HALO:, :], wu_ref[...])
    y = _causal_conv(cw_ref, gt_ref, HALO, tm, FFN_CONV) + cb_ref[...]
    act = (y * _sigmoid(y) * up).astype(BF16)
    o_ref[...] += _dot(act, wo_ref[...])


def _ffn_block(x, g_all, w_in_all, cw_all, cb_all, w_out_all, l, *, tm=1024, tf=512, name):
    s, d = x.shape
    tm = min(tm, s)
    nf = D_FF // tf
    nbytes = 4 * tm * d * 4 + (tm + HALO) * d * 2 + 6 * d * tf * 2 + 3 * (tm + HALO) * tf * 4
    return pl.pallas_call(
        _ffn_kernel, grid=(s // tm, nf), name=name,
        in_specs=[pl.BlockSpec((HALO, d), _halo_index(tm)),
                  pl.BlockSpec((tm, d), lambda i, j: (i, 0)),
                  pl.BlockSpec((None, 1, d), lambda i, j: (l, 0, 0)),
                  pl.BlockSpec((None, d, tf), lambda i, j: (l, 0, j)),
                  pl.BlockSpec((None, d, tf), lambda i, j: (l, 0, j + nf)),
                  pl.BlockSpec((None, FFN_CONV, tf), lambda i, j: (l, 0, j)),
                  pl.BlockSpec((None, 1, tf), lambda i, j: (l, 0, j)),
                  pl.BlockSpec((None, tf, d), lambda i, j: (l, j, 0))],
        out_specs=pl.BlockSpec((tm, d), lambda i, j: (i, 0)),
        out_shape=jax.ShapeDtypeStruct((s, d), F32),
        scratch_shapes=[pltpu.VMEM((tm + HALO, d), BF16), pltpu.VMEM((tm + HALO, tf), F32)],
        compiler_params=_params(("parallel", "arbitrary"), nbytes),
    )(x, x, g_all, w_in_all, w_in_all, cw_all, cb_all, w_out_all)


GDN_STEP_CHUNKS = 4
NEUMANN_DOUBLINGS = int(math.log2(GDN_CHUNK)) - 1


def _gdn_kernel(q_ref, k_ref, v_ref, z_ref, ba_ref, alog_ref, dt_ref, ng_ref, o_ref, st_ref):
    c = GDN_CHUNK
    rows = q_ref.shape[0]
    nchunks = rows // c

    @pl.when(pl.program_id(0) == 0)
    def _():
        st_ref[...] = jnp.zeros_like(st_ref)

    ba = ba_ref[...]
    lane = lax.broadcasted_iota(jnp.int32, ba.shape, 1)
    row_in_chunk = lax.broadcasted_iota(jnp.int32, ba.shape, 0) % c
    bg = jnp.where(lane < GDN_HEADS, _sigmoid(ba), -jnp.exp(alog_ref[...]) * _softplus(ba + dt_ref[...]))
    gcs = bg
    shift = 1
    while shift < c:
        gcs = gcs + jnp.where(row_in_chunk >= shift, pltpu.roll(gcs, shift, 0), 0.0)
        shift *= 2
    egcs = jnp.exp(gcs)

    ri = lax.broadcasted_iota(jnp.int32, (c, c), 0)
    ci = lax.broadcasted_iota(jnp.int32, (c, c), 1)
    causal = ri >= ci
    strict = ri > ci
    eye = (ri == ci).astype(F32)
    ng = ng_ref[...]

    heads = range(GDN_HEADS)
    chains = [(n, h) for n in range(nchunks) for h in heads]
    rsl = lambda n: slice(n * c, (n + 1) * c)
    hsl = lambda h: slice(h * GDN_DK, (h + 1) * GDN_DK)
    gct = [jnp.concatenate([gcs[rsl(n)], jnp.zeros((c, V7X_LANES), F32)], axis=0).T for n in range(nchunks)]

    kf, beta, gc_col, egc, g_last, decay, kb = {}, {}, {}, {}, {}, {}, {}
    for n, h in chains:
        kf[n, h] = k_ref[rsl(n), hsl(h)].astype(F32)
        beta[n, h] = bg[rsl(n), h:h + 1]
        gc_col[n, h] = gcs[rsl(n), GDN_HEADS + h:GDN_HEADS + h + 1]
        egc[n, h] = egcs[rsl(n), GDN_HEADS + h:GDN_HEADS + h + 1]
        gc_row = gct[n][GDN_HEADS + h:GDN_HEADS + h + 1, 0:c]
        g_last[n, h] = gc_row[:, c - 1:c]
        decay[n, h] = jnp.where(causal, jnp.exp(jnp.minimum(gc_col[n, h] - gc_row, 0.0)), 0.0)
        kb[n, h] = kf[n, h] * beta[n, h]
    qk = {ch: _dot_nt(jnp.concatenate([q_ref[rsl(ch[0]), hsl(ch[1])], kb[ch].astype(BF16)], axis=0),
                      k_ref[rsl(ch[0]), hsl(ch[1])]) for ch in chains}
    attn = {ch: (qk[ch][0:c] * decay[ch]).astype(BF16) for ch in chains}
    lmat = {ch: jnp.where(strict, qk[ch][c:2 * c] * decay[ch], 0.0) for ch in chains}
    pm = {ch: eye - lmat[ch] for ch in chains}
    mpow = {ch: lmat[ch].astype(BF16) for ch in chains}
    for _ in range(NEUMANN_DOUBLINGS):
        mpow = {ch: _dot(mpow[ch], mpow[ch]).astype(BF16) for ch in chains}
        pm = {ch: pm[ch] + _dot(pm[ch].astype(BF16), mpow[ch]) for ch in chains}
    uw = {ch: _dot(pm[ch].astype(BF16),
                   jnp.concatenate([v_ref[rsl(ch[0]), hsl(ch[1])].astype(F32) * beta[ch], kb[ch] * egc[ch]],
                                   axis=1).astype(BF16)) for ch in chains}
    wq = {ch: jnp.concatenate([uw[ch][:, GDN_DV:],
                               q_ref[rsl(ch[0]), hsl(ch[1])].astype(F32) * egc[ch]], axis=0).astype(BF16)
          for ch in chains}
    k_dec = {ch: (kf[ch] * jnp.exp(g_last[ch] - gc_col[ch])).astype(BF16) for ch in chains}

    state = [st_ref[h] for h in heads]
    for n in range(nchunks):
        ws = [_dot(wq[n, h], state[h].astype(BF16)) for h in heads]
        v_new = [(uw[n, h][:, 0:GDN_DV] - ws[h][0:c]).astype(BF16) for h in heads]
        state = [state[h] * jnp.exp(g_last[n, h]) + _dot_tn(k_dec[n, h], v_new[h]) for h in heads]
        o = [ws[h][c:2 * c] + _dot(attn[n, h], v_new[h]) for h in heads]
        for h in heads:
            o_ref[rsl(n), hsl(h)] = (_rms(o[h], ng) * z_ref[rsl(n), hsl(h)].astype(F32)).astype(o_ref.dtype)
    for h in heads:
        st_ref[h] = state[h]


def _gdn_mixer(p, ba, alog_all, dt_all, ng_all, e, *, name):
    s = p.shape[0]
    rows = GDN_STEP_CHUNKS * GDN_CHUNK
    col = lambda b: pl.BlockSpec((rows, GDN_V), lambda i: (i, b))
    layer = lambda shape: pl.BlockSpec((None,) + shape, lambda i: (e, 0, 0))
    return pl.pallas_call(
        _gdn_kernel, grid=(s // rows,), name=name,
        in_specs=[col(0), col(1), col(2), col(3),
                  pl.BlockSpec((rows, V7X_LANES), lambda i: (i, 0)),
                  layer((1, V7X_LANES)), layer((1, V7X_LANES)), layer((1, GDN_DV))],
        out_specs=pl.BlockSpec((rows, GDN_V), lambda i: (i, 0)),
        out_shape=jax.ShapeDtypeStruct((s, GDN_V), BF16),
        scratch_shapes=[pltpu.VMEM((GDN_HEADS, GDN_DK, GDN_DV), F32)],
        compiler_params=_params(("arbitrary",), 16 * rows * GDN_V * 4),
    )(p, p, p, p, ba, alog_all, dt_all, ng_all)


def _lru_kernel(xc_ref, gate_ref, wr_ref, br_ref, wi_ref, bi_ref, lam_ref, o_ref, a_ref, b_ref, carry_ref):
    tt = xc_ref.shape[0]

    @pl.when(pl.program_id(0) == 0)
    def _():
        carry_ref[...] = jnp.zeros_like(carry_ref)

    for blk in range(LRU_BLOCKS):
        sl = slice(blk * LRU_BLOCK, (blk + 1) * LRU_BLOCK)
        xb = xc_ref[:, sl]
        r = _sigmoid(_dot(xb, wr_ref[blk]) + br_ref[:, sl])
        i = _sigmoid(_dot(xb, wi_ref[blk]) + bi_ref[:, sl])
        log_a = (-LRU_C) * r * _softplus(-lam_ref[:, sl])
        a = jnp.exp(log_a)
        one_minus_a2 = -jnp.tanh(log_a) * (1.0 + a * a)
        a_ref[:, sl] = a
        b_ref[:, sl] = jnp.sqrt(one_minus_a2) * (i * xb.astype(F32))

    row = lax.broadcasted_iota(jnp.int32, (V7X_SUBLANES, LRU_WIDTH), 0)

    def group(gi, hprev):
        r0 = pl.multiple_of(gi * V7X_SUBLANES, V7X_SUBLANES)
        a = a_ref[pl.ds(r0, V7X_SUBLANES), :]
        b = b_ref[pl.ds(r0, V7X_SUBLANES), :]
        shift = 1
        while shift < V7X_SUBLANES:
            keep = row >= shift
            b = b + a * jnp.where(keep, pltpu.roll(b, shift, 0), 0.0)
            a = a * jnp.where(keep, pltpu.roll(a, shift, 0), 1.0)
            shift *= 2
        hcur = b + a * hprev
        b_ref[pl.ds(r0, V7X_SUBLANES), :] = hcur
        return jnp.broadcast_to(hcur[V7X_SUBLANES - 1:V7X_SUBLANES, :], hcur.shape)

    carry_ref[...] = lax.fori_loop(0, tt // V7X_SUBLANES, group, carry_ref[...])
    o_ref[...] = (b_ref[...] * gate_ref[...].astype(F32)).astype(o_ref.dtype)


def _lru_mixer(p, wr_all, br_all, wi_all, bi_all, lam_all, e, *, tt=256, name):
    s = p.shape[0]
    tt = min(tt, s)
    vec = pl.BlockSpec((None, 1, LRU_WIDTH), lambda i: (e, 0, 0))
    mat = pl.BlockSpec((None, LRU_BLOCKS, LRU_BLOCK, LRU_BLOCK), lambda i: (e, 0, 0, 0))
    return pl.pallas_call(
        _lru_kernel, grid=(s // tt,), name=name,
        in_specs=[pl.BlockSpec((tt, LRU_WIDTH), lambda i: (i, 4)),
                  pl.BlockSpec((tt, LRU_WIDTH), lambda i: (i, 5)),
                  mat, vec, mat, vec, vec],
        out_specs=pl.BlockSpec((tt, LRU_WIDTH), lambda i: (i, 0)),
        out_shape=jax.ShapeDtypeStruct((s, LRU_WIDTH), BF16),
        scratch_shapes=[pltpu.VMEM((tt, LRU_WIDTH), F32),
                        pltpu.VMEM((tt, LRU_WIDTH), F32),
                        pltpu.VMEM((V7X_SUBLANES, LRU_WIDTH), F32)],
        compiler_params=_params(("arbitrary",), 10 * tt * LRU_WIDTH * 4),
    )(p, p, wr_all, br_all, wi_all, bi_all, lam_all)


def _rope_kernel(pos_ref, inv_ref, cos_ref, sin_ref):
    ang = pos_ref[...].astype(F32) * inv_ref[...]
    cos_ref[...] = jnp.cos(ang)
    sin_ref[...] = jnp.sin(ang)


def _rope_tables(pos_col, inv_lanes, *, tm=1024):
    s = pos_col.shape[0]
    tm = min(tm, s)
    out = jax.ShapeDtypeStruct((s, V7X_LANES), F32)
    o_spec = pl.BlockSpec((tm, V7X_LANES), lambda i: (i, 0))
    return pl.pallas_call(
        _rope_kernel, grid=(s // tm,), name="rope_tables",
        in_specs=[pl.BlockSpec((tm, 1), lambda i: (i, 0)), pl.BlockSpec((1, V7X_LANES), lambda i: (0, 0))],
        out_specs=[o_spec, o_spec], out_shape=[out, out],
        compiler_params=_params(("parallel",), 8 * tm * V7X_LANES * 4),
    )(pos_col, inv_lanes)


SWA_PAIR = 2 * V7X_LANES
SWA_Q_PAIRS = SWA_Q // SWA_PAIR
SWA_KV_PAIRS = SWA_KV // SWA_PAIR
SWA_GROUPS_PER_PAIR = V7X_LANES // SWA_HALF


def _swa_kernel(sink_ref, qkv_ref, cos_ref, sin_ref, qg_ref, kg_ref, seg_ref, o_ref, k_ref, v_ref):
    blk = SWA_BLOCK
    first = pl.program_id(0) == 0

    @pl.when(first)
    def _():
        k_ref[0:blk, :] = jnp.zeros((blk, SWA_KV), BF16)
        v_ref[0:blk, :] = jnp.zeros((blk, SWA_KV), BF16)

    @pl.when(jnp.logical_not(first))
    def _():
        k_ref[0:blk, :] = k_ref[blk:2 * blk, :]
        v_ref[0:blk, :] = v_ref[blk:2 * blk, :]

    cos = cos_ref[...]
    sin = sin_ref[...]
    seg = seg_ref[...]

    def norm_rope(col0, gain_ref):
        a = qkv_ref[:, col0:col0 + V7X_LANES]
        b = qkv_ref[:, col0 + V7X_LANES:col0 + SWA_PAIR]
        sq = a * a + b * b
        hi = sq.astype(BF16)
        lo = (sq - hi.astype(F32)).astype(BF16)
        ss = _dot(hi, seg) + _dot(lo, seg)
        scale = lax.rsqrt(ss * (1.0 / SWA_HD) + EPS)
        a = a * scale * gain_ref[:, 0:V7X_LANES]
        b = b * scale * gain_ref[:, V7X_LANES:SWA_PAIR]
        return a * cos - b * sin, b * cos + a * sin

    for u in range(SWA_KV_PAIRS):
        ka, kb = norm_rope(SWA_Q + u * SWA_PAIR, kg_ref)
        k_ref[blk:2 * blk, u * SWA_PAIR:u * SWA_PAIR + V7X_LANES] = ka.astype(BF16)
        k_ref[blk:2 * blk, u * SWA_PAIR + V7X_LANES:(u + 1) * SWA_PAIR] = kb.astype(BF16)
    v_ref[blk:2 * blk, :] = qkv_ref[:, SWA_Q + SWA_KV:SWA_Q + 2 * SWA_KV].astype(BF16)

    qi = lax.broadcasted_iota(jnp.int32, (blk, 2 * blk), 0)
    kj = lax.broadcasted_iota(jnp.int32, (blk, 2 * blk), 1)
    lo_key = jnp.where(first, blk, 0)
    valid = (kj > qi) & (kj <= qi + SWA_WINDOW) & (kj >= lo_key)
    lane_group = lax.broadcasted_iota(jnp.int32, (blk, SWA_PAIR), 1) % V7X_LANES // SWA_HALF

    for t in range(SWA_Q_PAIRS):
        u = t // SWA_GROUPS_PER_PAIR
        qa, qb = norm_rope(t * SWA_PAIR, qg_ref)
        qp = jnp.concatenate([qa, qb], axis=1)
        qs = jnp.concatenate([jnp.where(lane_group == gl, qp, 0.0) for gl in range(SWA_GROUPS_PER_PAIR)],
                             axis=0).astype(BF16)
        kp = k_ref[:, u * SWA_PAIR:(u + 1) * SWA_PAIR]
        sc_all = _dot_nt(qs, kp) * (SWA_HD ** -0.5)
        ps, invs = [], []
        for gl in range(SWA_GROUPS_PER_PAIR):
            head = SWA_GROUP * (SWA_GROUPS_PER_PAIR * u + gl) + t % SWA_GROUPS_PER_PAIR
            sink = sink_ref[head]
            sc = jnp.where(valid, sc_all[gl * blk:(gl + 1) * blk], -jnp.inf)
            m = jnp.maximum(jnp.max(sc, axis=-1, keepdims=True), sink)
            p = jnp.exp(sc - m)
            invs.append(1.0 / (jnp.sum(p, axis=-1, keepdims=True) + jnp.exp(sink - m)))
            ps.append(p.astype(BF16))
        pv = _dot(jnp.concatenate(ps, axis=0), v_ref[:, u * SWA_PAIR:(u + 1) * SWA_PAIR])
        out = jnp.zeros((blk, SWA_PAIR), F32)
        for gl in range(SWA_GROUPS_PER_PAIR):
            out = jnp.where(lane_group == gl, pv[gl * blk:(gl + 1) * blk] * invs[gl], out)
        o_ref[:, t * SWA_PAIR:(t + 1) * SWA_PAIR] = out.astype(o_ref.dtype)


def _swa_mixer(qkv, sinks_all, cos, sin, qg_all, kg_all, seg, o, *, name):
    s, n = qkv.shape
    blk = SWA_BLOCK
    layer = lambda shape: pl.BlockSpec((None,) + shape, lambda i: (o, 0, 0))
    return pl.pallas_call(
        _swa_kernel, grid=(s // blk,), name=name,
        in_specs=[pl.BlockSpec(memory_space=pltpu.SMEM),
                  pl.BlockSpec((blk, n), lambda i: (i, 0)),
                  pl.BlockSpec((blk, V7X_LANES), lambda i: (i, 0)),
                  pl.BlockSpec((blk, V7X_LANES), lambda i: (i, 0)),
                  layer((1, SWA_PAIR)), layer((1, SWA_PAIR)),
                  pl.BlockSpec((V7X_LANES, V7X_LANES), lambda i: (0, 0))],
        out_specs=pl.BlockSpec((blk, SWA_Q), lambda i: (i, 0)),
        out_shape=jax.ShapeDtypeStruct((s, SWA_Q), BF16),
        scratch_shapes=[pltpu.VMEM((2 * blk, SWA_KV), BF16), pltpu.VMEM((2 * blk, SWA_KV), BF16)],
        compiler_params=_params(("arbitrary",), 8 * blk * n * 4),
    )(sinks_all[o], qkv, cos, sin, qg_all, kg_all, seg)


def _swa_pair_layout(w, axis, heads):
    g = SWA_GROUPS_PER_PAIR
    lead, tail = w.shape[:axis], w.shape[axis + 1:]
    nl = len(lead)
    if heads == SWA_HEADS:
        w = w.reshape(lead + (heads // (g * SWA_GROUP), g, SWA_GROUP, 2, SWA_HALF) + tail)
        perm = (0, 2, 3, 1, 4)
    else:
        w = w.reshape(lead + (heads // g, g, 2, SWA_HALF) + tail)
        perm = (0, 2, 1, 3)
    k = len(perm)
    full = tuple(range(nl)) + tuple(nl + p for p in perm) + tuple(range(nl + k, nl + k + len(tail)))
    return w.transpose(full).reshape(lead + (heads * SWA_HD,) + tail)


def _gain_lanes(g):
    n = g.shape[0]
    g = jnp.broadcast_to(g.reshape(n, 2, 1, SWA_HALF), (n, 2, SWA_GROUPS_PER_PAIR, SWA_HALF))
    return g.reshape(n, 1, SWA_PAIR).astype(F32)


def _lane_rows(vals, offset):
    n, m = vals.shape
    return jnp.pad(vals.astype(F32), ((0, 0), (offset, V7X_LANES - offset - m))).reshape(n, 1, V7X_LANES)


def kernel(x, mem, positions, norm_mix, norm_cross, norm_mem, norm_ffn, xq_w, xkv_w, xo_w, xq_norm, xk_norm, ffn_in_w, ffn_conv_w, ffn_conv_b, ffn_out_w, hyb_in_w, hyb_out_w, gdn_conv_w, gdn_a_log, gdn_dt_bias, gdn_norm, lru_conv_w, lru_conv_b, lru_wr, lru_br, lru_wi, lru_bi, lru_lambda, swa_in_w, swa_out_w, swa_q_norm, swa_k_norm, swa_sinks):
    bsz, s, d = x.shape
    assert bsz == 1 and d == D_MODEL and s % 1024 == 0
    xs = x[0]
    rows3 = lambda v: v.reshape(v.shape[0], 1, v.shape[1]).astype(F32)

    norm_mix3, norm_cross3, norm_ffn3 = rows3(norm_mix), rows3(norm_cross), rows3(norm_ffn)
    ba0 = GDN_QKV + GDN_V
    lru0 = ba0 + 2 * GDN_HEADS
    w_gdn = hyb_in_w[:, :, :ba0].astype(BF16)
    w_lru = hyb_in_w[:, :, lru0:].astype(BF16)
    w_ba = jnp.pad(hyb_in_w[:, :, ba0:lru0], ((0, 0), (0, 0), (0, V7X_LANES - 2 * GDN_HEADS))).astype(BF16)
    hyb_out_b = hyb_out_w.astype(BF16)
    alog3, dt3 = _lane_rows(gdn_a_log, GDN_HEADS), _lane_rows(gdn_dt_bias, GDN_HEADS)
    gdn_norm3 = rows3(gdn_norm)
    lru_cb3, lru_br3, lru_bi3, lru_lam3 = rows3(lru_conv_b), rows3(lru_br), rows3(lru_bi), rows3(lru_lambda)
    lru_wr_b, lru_wi_b = lru_wr.astype(BF16), lru_wi.astype(BF16)
    swa_in_b = jnp.concatenate(
        [_swa_pair_layout(swa_in_w[:, :, :SWA_Q], 2, SWA_HEADS),
         _swa_pair_layout(swa_in_w[:, :, SWA_Q:SWA_Q + SWA_KV], 2, SWA_KV_HEADS),
         _swa_pair_layout(swa_in_w[:, :, SWA_Q + SWA_KV:], 2, SWA_KV_HEADS)], axis=2).astype(BF16)
    swa_out_b = _swa_pair_layout(swa_out_w, 1, SWA_HEADS).astype(BF16)
    swa_qg, swa_kg = _gain_lanes(swa_q_norm), _gain_lanes(swa_k_norm)
    lanes = jnp.arange(V7X_LANES)
    seg = (lanes[:, None] // SWA_HALF == lanes[None, :] // SWA_HALF).astype(BF16)
    xq_b, xo_b = xq_w.astype(BF16), xo_w.astype(BF16)
    xq_norm3 = rows3(xq_norm)
    ffn_in_b, ffn_out_b = ffn_in_w.astype(BF16), ffn_out_w.astype(BF16)
    ffn_cb3 = rows3(ffn_conv_b)

    inv = 1.0 / (ROPE_THETA ** (jnp.arange(0, SWA_HD, 2, dtype=F32) / SWA_HD))
    inv_lanes = jnp.tile(inv, SWA_GROUPS_PER_PAIR).reshape(1, V7X_LANES)
    cos, sin = _rope_tables(positions.reshape(s, 1), inv_lanes)
    kn_all, v_all = _mem_kv(mem[0], rows3(norm_mem), xkv_w.astype(BF16), rows3(xk_norm))

    for l in range(DEPTH):
        if l % 2 == 0:
            e = l // 2
            p, ba = _hyb_in(xs, norm_mix3, w_gdn, w_lru, w_ba, gdn_conv_w, lru_conv_w, lru_cb3, l, e,
                            name=f"hyb_in_{l}")
            oa = _gdn_mixer(p, ba, alog3, dt3, gdn_norm3, e, name=f"gdn_{l}")
            ob = _lru_mixer(p, lru_wr_b, lru_br3, lru_wi_b, lru_bi3, lru_lam3, e, name=f"lru_{l}")
            xs = _matmul_res([oa, ob], hyb_out_b, e, xs, name=f"hyb_out_{l}")
        else:
            o = l // 2
            qkv = _norm_matmul(xs, norm_mix3, swa_in_b, l, o, name=f"swa_in_{l}")
            att = _swa_mixer(qkv, swa_sinks.astype(F32), cos, sin, swa_qg, swa_kg, seg, o, name=f"swa_{l}")
            xs = _matmul_res([att], swa_out_b, o, xs, name=f"swa_out_{l}")
        xs = _cross_block(xs, norm_cross3, xq_b, xq_norm3, kn_all, v_all, xo_b, l, name=f"cross_{l}")
        xs = _ffn_block(xs, norm_ffn3, ffn_in_b, ffn_conv_w, ffn_cb3, ffn_out_b, l, name=f"ffn_{l}")
    return xs[None]
```

```python
import math

import jax
import jax.numpy as jnp
from jax import lax
from jax.experimental import pallas as pl
from jax.experimental.pallas import tpu as pltpu

F32 = jnp.float32
BF16 = jnp.bfloat16

D_MODEL = 2048
DEPTH = 4
MEM_LEN = 256
EPS = 1e-6
GDN_HEADS = 8
GDN_DK = 128
GDN_DV = 128
GDN_CONV = 4
GDN_CHUNK = 64
GDN_QK = GDN_HEADS * GDN_DK
GDN_V = GDN_HEADS * GDN_DV
GDN_QKV = 2 * GDN_QK + GDN_V
LRU_WIDTH = D_MODEL // 2
LRU_BLOCKS = 8
LRU_BLOCK = LRU_WIDTH // LRU_BLOCKS
LRU_CONV = 4
LRU_C = 8.0
SWA_HEADS = 32
SWA_KV_HEADS = 8
SWA_GROUP = SWA_HEADS // SWA_KV_HEADS
SWA_HD = 64
SWA_HALF = SWA_HD // 2
SWA_WINDOW = 128
SWA_BLOCK = 128
SWA_Q = SWA_HEADS * SWA_HD
SWA_KV = SWA_KV_HEADS * SWA_HD
ROPE_THETA = 10000.0
X_HEADS = 4
X_HD = 128
X_INNER = X_HEADS * X_HD
D_FF = 5632
FFN_CONV = 3
LOG2E = math.log2(math.e)

V7X_LANES = 128
V7X_SUBLANES = 8
V7X_BF16_ROWS = 16
V7X_VMEM_BYTES = 64 * 1024 * 1024

HALO = V7X_BF16_ROWS


def _vmem_limit(nbytes):
    return int(min(max(2 * nbytes, 16 * 1024 * 1024), V7X_VMEM_BYTES - 8 * 1024 * 1024))


def _params(semantics, nbytes):
    return pltpu.CompilerParams(dimension_semantics=semantics, vmem_limit_bytes=_vmem_limit(nbytes))


def _rms(x, g):
    return x * lax.rsqrt(jnp.mean(x * x, axis=-1, keepdims=True) + EPS) * g


def _sigmoid(x):
    return 1.0 / (1.0 + jnp.exp(-x))


def _softplus(x):
    return jnp.maximum(x, 0.0) + jnp.log1p(jnp.exp(-jnp.abs(x)))


def _gelu_tanh(x):
    return 0.5 * x * (1.0 + jnp.tanh(math.sqrt(2.0 / math.pi) * (x + 0.044715 * (x * x * x))))


def _dot(a, b):
    return jnp.dot(a, b, preferred_element_type=F32)


def _dot_nt(a, b):
    return lax.dot_general(a, b, (((1,), (1,)), ((), ())), preferred_element_type=F32)


def _dot_tn(a, b):
    return lax.dot_general(a, b, (((0,), (0,)), ((), ())), preferred_element_type=F32)


def _halo_index(tm):
    per_tile = tm // HALO
    return lambda i, j: (jnp.maximum(i * per_tile - 1, 0), 0)


def _causal_conv(cw_ref, src_ref, row0, rows, width):
    y = cw_ref[width - 1:width, :] * src_ref[row0:row0 + rows, :]
    for j in range(width - 1):
        back = width - 1 - j
        y = y + cw_ref[j:j + 1, :] * src_ref[row0 - back:row0 - back + rows, :]
    return y


def _norm_matmul_kernel(x_ref, g_ref, w_ref, o_ref, h_ref):
    @pl.when(pl.program_id(1) == 0)
    def _():
        h_ref[...] = _rms(x_ref[...], g_ref[...]).astype(BF16)

    o_ref[...] = _dot(h_ref[...], w_ref[...]).astype(o_ref.dtype)


def _norm_matmul(x, g_all, w_all, l, wl, *, tm=1024, tn=1024, name):
    s, d = x.shape
    n = w_all.shape[-1]
    tm, tn = min(tm, s), min(tn, n)
    nbytes = 2 * tm * d * 4 + tm * d * 2 + 2 * d * tn * 2 + 2 * tm * tn * 4
    return pl.pallas_call(
        _norm_matmul_kernel, grid=(s // tm, n // tn), name=name,
        in_specs=[pl.BlockSpec((tm, d), lambda i, j: (i, 0)),
                  pl.BlockSpec((None, 1, d), lambda i, j: (l, 0, 0)),
                  pl.BlockSpec((None, d, tn), lambda i, j: (wl, 0, j))],
        out_specs=pl.BlockSpec((tm, tn), lambda i, j: (i, j)),
        out_shape=jax.ShapeDtypeStruct((s, n), F32),
        scratch_shapes=[pltpu.VMEM((tm, d), BF16)],
        compiler_params=_params(("parallel", "arbitrary"), nbytes),
    )(x, g_all, w_all)


HYB_TN = 1024
HYB_BLOCKS = 6
HYB_EPI_ROWS = 256


HYB_PIECE = 2 * V7X_LANES


def _hyb_in_kernel(xh_ref, x_ref, g_ref, wg_ref, wl_ref, wba_ref, gcw_ref, lcw_ref, lcb_ref,
                   p_ref, ba_ref, h_ref, *pe_refs):
    tm = x_ref.shape[0]
    j = pl.program_id(1)
    tiles = range(0, tm, HYB_EPI_ROWS)
    npieces = len(pe_refs)

    @pl.when(j == 0)
    def _():
        g = g_ref[...]
        keep = (pl.program_id(0) > 0).astype(F32)
        h_ref[0:HALO, :] = _rms(xh_ref[...] * keep, g).astype(BF16)
        hb = _rms(x_ref[...], g).astype(BF16)
        h_ref[HALO:, :] = hb
        ba_ref[...] = _dot(hb, wba_ref[...])

    def matmul(blk, piece):
        w_ref = wg_ref if blk < 4 else wl_ref
        pe_refs[piece][...] = _dot(h_ref[...], w_ref[:, piece * HYB_PIECE:(piece + 1) * HYB_PIECE])

    def epilogue(blk, piece):
        src = pe_refs[piece]
        cols = slice(piece * HYB_PIECE, (piece + 1) * HYB_PIECE)
        for r0 in tiles:
            rows = slice(r0, r0 + HYB_EPI_ROWS)
            if blk <= 2:
                y = _causal_conv(gcw_ref.at[:, cols], src, HALO + r0, HYB_EPI_ROWS, GDN_CONV)
                y = y * _sigmoid(y)
                if blk == 2:
                    p_ref[rows, cols] = y.astype(p_ref.dtype)
                    continue
                qscale = GDN_DK ** -0.5 if blk == 0 else 1.0
                for h in range(HYB_PIECE // GDN_DK):
                    yh = y[:, h * GDN_DK:(h + 1) * GDN_DK]
                    inv = lax.rsqrt(jnp.sum(yh * yh, axis=-1, keepdims=True) + EPS) * qscale
                    c0 = piece * HYB_PIECE + h * GDN_DK
                    p_ref[rows, c0:c0 + GDN_DK] = (yh * inv).astype(p_ref.dtype)
            elif blk == 3:
                z = src[HALO + r0:HALO + r0 + HYB_EPI_ROWS, :]
                p_ref[rows, cols] = (z * _sigmoid(z)).astype(p_ref.dtype)
            elif blk == 4:
                xc = _causal_conv(lcw_ref.at[:, cols], src, HALO + r0, HYB_EPI_ROWS, LRU_CONV) + lcb_ref[:, cols]
                p_ref[rows, cols] = xc.astype(p_ref.dtype)
            else:
                gate = src[HALO + r0:HALO + r0 + HYB_EPI_ROWS, :]
                p_ref[rows, cols] = _gelu_tanh(gate).astype(p_ref.dtype)

    for blk in range(HYB_BLOCKS):
        @pl.when(j == blk)
        def _(blk=blk):
            matmul(blk, 0)
            for piece in range(npieces):
                if piece + 1 < npieces:
                    matmul(blk, piece + 1)
                epilogue(blk, piece)


def _hyb_in(x, g_all, w_gdn, w_lru, w_ba, gdn_cw, lru_cw, lru_cb, l, e, *, tm=512, name):
    s, d = x.shape
    tm, tn = min(tm, s), HYB_TN
    nbytes = (2 * tm * d * 4 + (tm + HALO) * d * 2 + 4 * d * tn * 2 + (tm + HALO) * tn * 4
              + 2 * tm * tn * 2 + 8 * HYB_EPI_ROWS * HYB_PIECE * 4)
    return pl.pallas_call(
        _hyb_in_kernel, grid=(s // tm, HYB_BLOCKS), name=name,
        in_specs=[pl.BlockSpec((HALO, d), _halo_index(tm)),
                  pl.BlockSpec((tm, d), lambda i, j: (i, 0)),
                  pl.BlockSpec((None, 1, d), lambda i, j: (l, 0, 0)),
                  pl.BlockSpec((None, d, tn), lambda i, j: (e, 0, jnp.minimum(j, 3))),
                  pl.BlockSpec((None, d, tn), lambda i, j: (e, 0, jnp.maximum(j - 4, 0))),
                  pl.BlockSpec((None, d, V7X_LANES), lambda i, j: (e, 0, 0)),
                  pl.BlockSpec((None, GDN_CONV, tn), lambda i, j: (e, 0, jnp.minimum(j, 2))),
                  pl.BlockSpec((None, LRU_CONV, LRU_WIDTH), lambda i, j: (e, 0, 0)),
                  pl.BlockSpec((None, 1, LRU_WIDTH), lambda i, j: (e, 0, 0))],
        out_specs=[pl.BlockSpec((tm, tn), lambda i, j: (i, j)),
                   pl.BlockSpec((tm, V7X_LANES), lambda i, j: (i, 0))],
        out_shape=[jax.ShapeDtypeStruct((s, HYB_BLOCKS * tn), BF16),
                   jax.ShapeDtypeStruct((s, V7X_LANES), F32)],
        scratch_shapes=[pltpu.VMEM((tm + HALO, d), BF16)]
        + [pltpu.VMEM((tm + HALO, HYB_PIECE), F32) for _ in range(tn // HYB_PIECE)],
        compiler_params=_params(("parallel", "arbitrary"), nbytes),
    )(x, x, g_all, w_gdn, w_lru, w_ba, gdn_cw, lru_cw, lru_cb)


def _mem_kv_kernel(mem_ref, g_ref, w_ref, kg_ref, k_ref, v_ref):
    mn = _rms(mem_ref[...], g_ref[...]).astype(BF16)
    kv = _dot(mn, w_ref[...])
    kg = kg_ref[...]
    for h in range(X_HEADS):
        kh = kv[:, h * X_HD:(h + 1) * X_HD]
        k_ref[:, h * X_HD:(h + 1) * X_HD] = _rms(kh, kg).astype(BF16)
    v_ref[...] = kv[:, X_INNER:].astype(BF16)


def _mem_kv(mem, norm_mem, xkv_w, xk_norm):
    m, d = mem.shape
    nl = norm_mem.shape[0]
    out = jax.ShapeDtypeStruct((nl, m, X_INNER), BF16)
    o_spec = pl.BlockSpec((None, m, X_INNER), lambda l: (l, 0, 0))
    return pl.pallas_call(
        _mem_kv_kernel, grid=(nl,), name="mem_kv",
        in_specs=[pl.BlockSpec((m, d), lambda l: (0, 0)),
                  pl.BlockSpec((None, 1, d), lambda l: (l, 0, 0)),
                  pl.BlockSpec((None, d, 2 * X_INNER), lambda l: (l, 0, 0)),
                  pl.BlockSpec((None, 1, X_HD), lambda l: (l, 0, 0))],
        out_specs=[o_spec, o_spec], out_shape=[out, out],
        compiler_params=_params(("parallel",), 2 * d * 2 * X_INNER * 2 + m * d * 4),
    )(mem, norm_mem, xkv_w, xk_norm)


def _mix_cross_kernel(*refs):
    *a_refs, wm_ref, x_ref, g_ref, wq_ref, qg_ref, k_ref, v_ref, wo_ref, o_ref = refs
    x = x_ref[...]
    k0 = 0
    for a_ref in a_refs:
        ka = a_ref.shape[1]
        x = x + _dot(a_ref[...], wm_ref[k0:k0 + ka, :])
        k0 += ka
    h = _rms(x, g_ref[...]).astype(BF16)
    q = _dot(h, wq_ref[...])
    qg = qg_ref[...]
    outs = []
    for hh in range(X_HEADS):
        sl = slice(hh * X_HD, (hh + 1) * X_HD)
        qn = (_rms(q[:, sl], qg) * (X_HD ** -0.5 * LOG2E)).astype(BF16)
        sc = _dot_nt(qn, k_ref[:, sl])
        p = jnp.exp2(sc - jnp.max(sc, axis=-1, keepdims=True))
        inv = 1.0 / jnp.sum(p, axis=-1, keepdims=True)
        outs.append((_dot(p.astype(BF16), v_ref[:, sl]) * inv).astype(BF16))
    o = jnp.concatenate(outs, axis=-1)
    o_ref[...] = x + _dot(o, wo_ref[...])


def _mix_cross_block(a_list, wm_all, wl, x, g_all, wq_all, qg_all, kn_all, v_all, wo_all, l, *, tm=512, name):
    s, d = x.shape
    tm = min(tm, s)
    m = kn_all.shape[1]
    kmix = wm_all.shape[1]
    nbytes = (4 * tm * d * 4 + 2 * tm * kmix * 2 + 2 * kmix * d * 2 + 4 * d * X_INNER * 2
              + 4 * m * X_INNER * 2 + 4 * tm * m * 4)
    layer = lambda shape: pl.BlockSpec((None,) + shape, lambda i: (l, 0, 0))
    return pl.pallas_call(
        _mix_cross_kernel, grid=(s // tm,), name=name,
        in_specs=[pl.BlockSpec((tm, a.shape[1]), lambda i: (i, 0)) for a in a_list]
        + [pl.BlockSpec((None, kmix, d), lambda i: (wl, 0, 0)),
           pl.BlockSpec((tm, d), lambda i: (i, 0)), layer((1, d)), layer((d, X_INNER)),
           layer((1, X_HD)), layer((m, X_INNER)), layer((m, X_INNER)), layer((X_INNER, d))],
        out_specs=pl.BlockSpec((tm, d), lambda i: (i, 0)),
        out_shape=jax.ShapeDtypeStruct((s, d), F32),
        compiler_params=_params(("parallel",), nbytes),
    )(*a_list, wm_all, x, g_all, wq_all, qg_all, kn_all, v_all, wo_all)


def _ffn_kernel(xh_ref, x_ref, g_ref, wg_ref, wu_ref, cw_ref, cb_ref, wo_ref, o_ref, h_ref, gt_ref):
    tm = x_ref.shape[0]

    @pl.when(pl.program_id(1) == 0)
    def _():
        g = g_ref[...]
        keep = (pl.program_id(0) > 0).astype(F32)
        h_ref[0:HALO, :] = _rms(xh_ref[...] * keep, g).astype(BF16)
        x = x_ref[...]
        h_ref[HALO:, :] = _rms(x, g).astype(BF16)
        o_ref[...] = x

    gt_ref[...] = _dot(h_ref[...], wg_ref[...])
    up = _dot(h_ref[HALO:, :], wu_ref[...])
    y = _causal_conv(cw_ref, gt_ref, HALO, tm, FFN_CONV) + cb_ref[...]
    act = (y * _sigmoid(y) * up).astype(BF16)
    o_ref[...] += _dot(act, wo_ref[...])


def _ffn_block(x, g_all, w_in_all, cw_all, cb_all, w_out_all, l, *, tm=1024, tf=512, name):
    s, d = x.shape
    tm = min(tm, s)
    nf = D_FF // tf
    nbytes = 4 * tm * d * 4 + (tm + HALO) * d * 2 + 6 * d * tf * 2 + 3 * (tm + HALO) * tf * 4
    return pl.pallas_call(
        _ffn_kernel, grid=(s // tm, nf), name=name,
        in_specs=[pl.BlockSpec((HALO, d), _halo_index(tm)),
                  pl.BlockSpec((tm, d), lambda i, j: (i, 0)),
                  pl.BlockSpec((None, 1, d), lambda i, j: (l, 0, 0)),
                  pl.BlockSpec((None, d, tf), lambda i, j: (l, 0, j)),
                  pl.BlockSpec((None, d, tf), lambda i, j: (l, 0, j + nf)),
                  pl.BlockSpec((None, FFN_CONV, tf), lambda i, j: (l, 0, j)),
                  pl.BlockSpec((None, 1, tf), lambda i, j: (l, 0, j)),
                  pl.BlockSpec((None, tf, d), lambda i, j: (l, j, 0))],
        out_specs=pl.BlockSpec((tm, d), lambda i, j: (i, 0)),
        out_shape=jax.ShapeDtypeStruct((s, d), F32),
        scratch_shapes=[pltpu.VMEM((tm + HALO, d), BF16), pltpu.VMEM((tm + HALO, tf), F32)],
        compiler_params=_params(("parallel", "arbitrary"), nbytes),
    )(x, x, g_all, w_in_all, w_in_all, cw_all, cb_all, w_out_all)


GDN_STEP_CHUNKS = 4
NEUMANN_DOUBLINGS = int(math.log2(GDN_CHUNK)) - 1


def _gdn_kernel(q_ref, k_ref, v_ref, z_ref, ba_ref, alog_ref, dt_ref, ng_ref, o_ref, st_ref):
    c = GDN_CHUNK
    rows = q_ref.shape[0]
    nchunks = rows // c

    @pl.when(pl.program_id(0) == 0)
    def _():
        st_ref[...] = jnp.zeros_like(st_ref)

    ba = ba_ref[...]
    lane = lax.broadcasted_iota(jnp.int32, ba.shape, 1)
    row_in_chunk = lax.broadcasted_iota(jnp.int32, ba.shape, 0) % c
    bg = jnp.where(lane < GDN_HEADS, _sigmoid(ba), -jnp.exp(alog_ref[...]) * _softplus(ba + dt_ref[...]))
    gcs = bg
    shift = 1
    while shift < c:
        gcs = gcs + jnp.where(row_in_chunk >= shift, pltpu.roll(gcs, shift, 0), 0.0)
        shift *= 2
    egcs = jnp.exp(gcs)

    ri = lax.broadcasted_iota(jnp.int32, (c, c), 0)
    ci = lax.broadcasted_iota(jnp.int32, (c, c), 1)
    causal = ri >= ci
    strict = ri > ci
    eye = (ri == ci).astype(F32)
    ng = ng_ref[...]

    heads = range(GDN_HEADS)
    chains = [(n, h) for n in range(nchunks) for h in heads]
    rsl = lambda n: slice(n * c, (n + 1) * c)
    hsl = lambda h: slice(h * GDN_DK, (h + 1) * GDN_DK)
    gct = [jnp.concatenate([gcs[rsl(n)], jnp.zeros((c, V7X_LANES), F32)], axis=0).T for n in range(nchunks)]

    kf, beta, gc_col, egc, g_last, decay, kb = {}, {}, {}, {}, {}, {}, {}
    for n, h in chains:
        kf[n, h] = k_ref[rsl(n), hsl(h)].astype(F32)
        beta[n, h] = bg[rsl(n), h:h + 1]
        gc_col[n, h] = gcs[rsl(n), GDN_HEADS + h:GDN_HEADS + h + 1]
        egc[n, h] = egcs[rsl(n), GDN_HEADS + h:GDN_HEADS + h + 1]
        gc_row = gct[n][GDN_HEADS + h:GDN_HEADS + h + 1, 0:c]
        g_last[n, h] = gc_row[:, c - 1:c]
        decay[n, h] = jnp.where(causal, jnp.exp(jnp.minimum(gc_col[n, h] - gc_row, 0.0)), 0.0)
        kb[n, h] = kf[n, h] * beta[n, h]
    qk = {ch: _dot_nt(jnp.concatenate([q_ref[rsl(ch[0]), hsl(ch[1])], kb[ch].astype(BF16)], axis=0),
                      k_ref[rsl(ch[0]), hsl(ch[1])]) for ch in chains}
    attn = {ch: (qk[ch][0:c] * decay[ch]).astype(BF16) for ch in chains}
    lmat = {ch: jnp.where(strict, qk[ch][c:2 * c] * decay[ch], 0.0) for ch in chains}
    pm = {ch: eye - lmat[ch] for ch in chains}
    mpow = {ch: lmat[ch].astype(BF16) for ch in chains}
    for _ in range(NEUMANN_DOUBLINGS):
        mpow = {ch: _dot(mpow[ch], mpow[ch]).astype(BF16) for ch in chains}
        pm = {ch: pm[ch] + _dot(pm[ch].astype(BF16), mpow[ch]) for ch in chains}
    uw = {ch: _dot(pm[ch].astype(BF16),
                   jnp.concatenate([v_ref[rsl(ch[0]), hsl(ch[1])].astype(F32) * beta[ch], kb[ch] * egc[ch]],
                                   axis=1).astype(BF16)) for ch in chains}
    wq = {ch: jnp.concatenate([uw[ch][:, GDN_DV:],
                               q_ref[rsl(ch[0]), hsl(ch[1])].astype(F32) * egc[ch]], axis=0).astype(BF16)
          for ch in chains}
    k_dec = {ch: (kf[ch] * jnp.exp(g_last[ch] - gc_col[ch])).astype(BF16) for ch in chains}

    state = [st_ref[h] for h in heads]
    for n in range(nchunks):
        ws = [_dot(wq[n, h], state[h].astype(BF16)) for h in heads]
        v_new = [(uw[n, h][:, 0:GDN_DV] - ws[h][0:c]).astype(BF16) for h in heads]
        state = [state[h] * jnp.exp(g_last[n, h]) + _dot_tn(k_dec[n, h], v_new[h]) for h in heads]
        o = [ws[h][c:2 * c] + _dot(attn[n, h], v_new[h]) for h in heads]
        for h in heads:
            o_ref[rsl(n), hsl(h)] = (_rms(o[h], ng) * z_ref[rsl(n), hsl(h)].astype(F32)).astype(o_ref.dtype)
    for h in heads:
        st_ref[h] = state[h]


def _gdn_mixer(p, ba, alog_all, dt_all, ng_all, e, *, name):
    s = p.shape[0]
    rows = GDN_STEP_CHUNKS * GDN_CHUNK
    col = lambda b: pl.BlockSpec((rows, GDN_V), lambda i: (i, b))
    layer = lambda shape: pl.BlockSpec((None,) + shape, lambda i: (e, 0, 0))
    return pl.pallas_call(
        _gdn_kernel, grid=(s // rows,), name=name,
        in_specs=[col(0), col(1), col(2), col(3),
                  pl.BlockSpec((rows, V7X_LANES), lambda i: (i, 0)),
                  layer((1, V7X_LANES)), layer((1, V7X_LANES)), layer((1, GDN_DV))],
        out_specs=pl.BlockSpec((rows, GDN_V), lambda i: (i, 0)),
        out_shape=jax.ShapeDtypeStruct((s, GDN_V), BF16),
        scratch_shapes=[pltpu.VMEM((GDN_HEADS, GDN_DK, GDN_DV), F32)],
        compiler_params=_params(("arbitrary",), 16 * rows * GDN_V * 4),
    )(p, p, p, p, ba, alog_all, dt_all, ng_all)


def _lru_kernel(xc_ref, gate_ref, wr_ref, br_ref, wi_ref, bi_ref, lam_ref, o_ref, a_ref, b_ref, carry_ref):
    tt = xc_ref.shape[0]

    @pl.when(pl.program_id(0) == 0)
    def _():
        carry_ref[...] = jnp.zeros_like(carry_ref)

    for blk in range(LRU_BLOCKS):
        sl = slice(blk * LRU_BLOCK, (blk + 1) * LRU_BLOCK)
        xb = xc_ref[:, sl]
        r = _sigmoid(_dot(xb, wr_ref[blk]) + br_ref[:, sl])
        i = _sigmoid(_dot(xb, wi_ref[blk]) + bi_ref[:, sl])
        log_a = (-LRU_C) * r * _softplus(-lam_ref[:, sl])
        a = jnp.exp(log_a)
        one_minus_a2 = -jnp.tanh(log_a) * (1.0 + a * a)
        a_ref[:, sl] = a
        b_ref[:, sl] = jnp.sqrt(one_minus_a2) * (i * xb.astype(F32))

    row = lax.broadcasted_iota(jnp.int32, (V7X_SUBLANES, LRU_WIDTH), 0)

    def group(gi, hprev):
        r0 = pl.multiple_of(gi * V7X_SUBLANES, V7X_SUBLANES)
        a = a_ref[pl.ds(r0, V7X_SUBLANES), :]
        b = b_ref[pl.ds(r0, V7X_SUBLANES), :]
        shift = 1
        while shift < V7X_SUBLANES:
            keep = row >= shift
            b = b + a * jnp.where(keep, pltpu.roll(b, shift, 0), 0.0)
            a = a * jnp.where(keep, pltpu.roll(a, shift, 0), 1.0)
            shift *= 2
        hcur = b + a * hprev
        b_ref[pl.ds(r0, V7X_SUBLANES), :] = hcur
        return jnp.broadcast_to(hcur[V7X_SUBLANES - 1:V7X_SUBLANES, :], hcur.shape)

    carry_ref[...] = lax.fori_loop(0, tt // V7X_SUBLANES, group, carry_ref[...])
    o_ref[...] = (b_ref[...] * gate_ref[...].astype(F32)).astype(o_ref.dtype)


def _lru_mixer(p, wr_all, br_all, wi_all, bi_all, lam_all, e, *, tt=256, name):
    s = p.shape[0]
    tt = min(tt, s)
    vec = pl.BlockSpec((None, 1, LRU_WIDTH), lambda i: (e, 0, 0))
    mat = pl.BlockSpec((None, LRU_BLOCKS, LRU_BLOCK, LRU_BLOCK), lambda i: (e, 0, 0, 0))
    return pl.pallas_call(
        _lru_kernel, grid=(s // tt,), name=name,
        in_specs=[pl.BlockSpec((tt, LRU_WIDTH), lambda i: (i, 4)),
                  pl.BlockSpec((tt, LRU_WIDTH), lambda i: (i, 5)),
                  mat, vec, mat, vec, vec],
        out_specs=pl.BlockSpec((tt, LRU_WIDTH), lambda i: (i, 0)),
        out_shape=jax.ShapeDtypeStruct((s, LRU_WIDTH), BF16),
        scratch_shapes=[pltpu.VMEM((tt, LRU_WIDTH), F32),
                        pltpu.VMEM((tt, LRU_WIDTH), F32),
                        pltpu.VMEM((V7X_SUBLANES, LRU_WIDTH), F32)],
        compiler_params=_params(("arbitrary",), 10 * tt * LRU_WIDTH * 4),
    )(p, p, wr_all, br_all, wi_all, bi_all, lam_all)


def _rope_kernel(pos_ref, inv_ref, cos_ref, sin_ref):
    ang = pos_ref[...].astype(F32) * inv_ref[...]
    cos_ref[...] = jnp.cos(ang)
    sin_ref[...] = jnp.sin(ang)


def _rope_tables(pos_col, inv_lanes, *, tm=1024):
    s = pos_col.shape[0]
    tm = min(tm, s)
    out = jax.ShapeDtypeStruct((s, V7X_LANES), F32)
    o_spec = pl.BlockSpec((tm, V7X_LANES), lambda i: (i, 0))
    return pl.pallas_call(
        _rope_kernel, grid=(s // tm,), name="rope_tables",
        in_specs=[pl.BlockSpec((tm, 1), lambda i: (i, 0)), pl.BlockSpec((1, V7X_LANES), lambda i: (0, 0))],
        out_specs=[o_spec, o_spec], out_shape=[out, out],
        compiler_params=_params(("parallel",), 8 * tm * V7X_LANES * 4),
    )(pos_col, inv_lanes)


SWA_PAIR = 2 * V7X_LANES
SWA_Q_PAIRS = SWA_Q // SWA_PAIR
SWA_KV_PAIRS = SWA_KV // SWA_PAIR
SWA_GROUPS_PER_PAIR = V7X_LANES // SWA_HALF


def _swa_kernel(sink_ref, qkv_ref, cos_ref, sin_ref, qg_ref, kg_ref, seg_ref, o_ref, k_ref, v_ref):
    blk = SWA_BLOCK
    first = pl.program_id(0) == 0

    @pl.when(first)
    def _():
        k_ref[0:blk, :] = jnp.zeros((blk, SWA_KV), BF16)
        v_ref[0:blk, :] = jnp.zeros((blk, SWA_KV), BF16)

    @pl.when(jnp.logical_not(first))
    def _():
        k_ref[0:blk, :] = k_ref[blk:2 * blk, :]
        v_ref[0:blk, :] = v_ref[blk:2 * blk, :]

    cos = cos_ref[...]
    sin = sin_ref[...]
    seg = seg_ref[...]

    def norm_rope(col0, gain_ref):
        a = qkv_ref[:, col0:col0 + V7X_LANES]
        b = qkv_ref[:, col0 + V7X_LANES:col0 + SWA_PAIR]
        sq = a * a + b * b
        hi = sq.astype(BF16)
        lo = (sq - hi.astype(F32)).astype(BF16)
        ss = _dot(hi, seg) + _dot(lo, seg)
        scale = lax.rsqrt(ss * (1.0 / SWA_HD) + EPS)
        a = a * scale * gain_ref[:, 0:V7X_LANES]
        b = b * scale * gain_ref[:, V7X_LANES:SWA_PAIR]
        return a * cos - b * sin, b * cos + a * sin

    for u in range(SWA_KV_PAIRS):
        ka, kb = norm_rope(SWA_Q + u * SWA_PAIR, kg_ref)
        k_ref[blk:2 * blk, u * SWA_PAIR:u * SWA_PAIR + V7X_LANES] = ka.astype(BF16)
        k_ref[blk:2 * blk, u * SWA_PAIR + V7X_LANES:(u + 1) * SWA_PAIR] = kb.astype(BF16)
    v_ref[blk:2 * blk, :] = qkv_ref[:, SWA_Q + SWA_KV:SWA_Q + 2 * SWA_KV].astype(BF16)

    qi = lax.broadcasted_iota(jnp.int32, (blk, 2 * blk), 0)
    kj = lax.broadcasted_iota(jnp.int32, (blk, 2 * blk), 1)
    lo_key = jnp.where(first, blk, 0)
    valid = (kj > qi) & (kj <= qi + SWA_WINDOW) & (kj >= lo_key)
    lane_group = lax.broadcasted_iota(jnp.int32, (blk, SWA_PAIR), 1) % V7X_LANES // SWA_HALF

    for t in range(SWA_Q_PAIRS):
        u = t // SWA_GROUPS_PER_PAIR
        qa, qb = norm_rope(t * SWA_PAIR, qg_ref)
        qp = jnp.concatenate([qa, qb], axis=1) * (SWA_HD ** -0.5 * LOG2E)
        qs = jnp.concatenate([jnp.where(lane_group == gl, qp, 0.0) for gl in range(SWA_GROUPS_PER_PAIR)],
                             axis=0).astype(BF16)
        kp = k_ref[:, u * SWA_PAIR:(u + 1) * SWA_PAIR]
        sc_all = _dot_nt(qs, kp)
        ps, invs = [], []
        for gl in range(SWA_GROUPS_PER_PAIR):
            head = SWA_GROUP * (SWA_GROUPS_PER_PAIR * u + gl) + t % SWA_GROUPS_PER_PAIR
            sink = sink_ref[head] * LOG2E
            sc = jnp.where(valid, sc_all[gl * blk:(gl + 1) * blk], -jnp.inf)
            m = jnp.maximum(jnp.max(sc, axis=-1, keepdims=True), sink)
            p = jnp.exp2(sc - m)
            invs.append(1.0 / (jnp.sum(p, axis=-1, keepdims=True) + jnp.exp2(sink - m)))
            ps.append(p.astype(BF16))
        pv = _dot(jnp.concatenate(ps, axis=0), v_ref[:, u * SWA_PAIR:(u + 1) * SWA_PAIR])
        out = jnp.zeros((blk, SWA_PAIR), F32)
        for gl in range(SWA_GROUPS_PER_PAIR):
            out = jnp.where(lane_group == gl, pv[gl * blk:(gl + 1) * blk] * invs[gl], out)
        o_ref[:, t * SWA_PAIR:(t + 1) * SWA_PAIR] = out.astype(o_ref.dtype)


def _swa_mixer(qkv, sinks_all, cos, sin, qg_all, kg_all, seg, o, *, name):
    s, n = qkv.shape
    blk = SWA_BLOCK
    layer = lambda shape: pl.BlockSpec((None,) + shape, lambda i: (o, 0, 0))
    return pl.pallas_call(
        _swa_kernel, grid=(s // blk,), name=name,
        in_specs=[pl.BlockSpec(memory_space=pltpu.SMEM),
                  pl.BlockSpec((blk, n), lambda i: (i, 0)),
                  pl.BlockSpec((blk, V7X_LANES), lambda i: (i, 0)),
                  pl.BlockSpec((blk, V7X_LANES), lambda i: (i, 0)),
                  layer((1, SWA_PAIR)), layer((1, SWA_PAIR)),
                  pl.BlockSpec((V7X_LANES, V7X_LANES), lambda i: (0, 0))],
        out_specs=pl.BlockSpec((blk, SWA_Q), lambda i: (i, 0)),
        out_shape=jax.ShapeDtypeStruct((s, SWA_Q), BF16),
        scratch_shapes=[pltpu.VMEM((2 * blk, SWA_KV), BF16), pltpu.VMEM((2 * blk, SWA_KV), BF16)],
        compiler_params=_params(("arbitrary",), 8 * blk * n * 4),
    )(sinks_all[o], qkv, cos, sin, qg_all, kg_all, seg)


def _swa_pair_layout(w, axis, heads):
    g = SWA_GROUPS_PER_PAIR
    lead, tail = w.shape[:axis], w.shape[axis + 1:]
    nl = len(lead)
    if heads == SWA_HEADS:
        w = w.reshape(lead + (heads // (g * SWA_GROUP), g, SWA_GROUP, 2, SWA_HALF) + tail)
        perm = (0, 2, 3, 1, 4)
    else:
        w = w.reshape(lead + (heads // g, g, 2, SWA_HALF) + tail)
        perm = (0, 2, 1, 3)
    k = len(perm)
    full = tuple(range(nl)) + tuple(nl + p for p in perm) + tuple(range(nl + k, nl + k + len(tail)))
    return w.transpose(full).reshape(lead + (heads * SWA_HD,) + tail)


def _gain_lanes(g):
    n = g.shape[0]
    g = jnp.broadcast_to(g.reshape(n, 2, 1, SWA_HALF), (n, 2, SWA_GROUPS_PER_PAIR, SWA_HALF))
    return g.reshape(n, 1, SWA_PAIR).astype(F32)


def _lane_rows(vals, offset):
    n, m = vals.shape
    return jnp.pad(vals.astype(F32), ((0, 0), (offset, V7X_LANES - offset - m))).reshape(n, 1, V7X_LANES)


def kernel(x, mem, positions, norm_mix, norm_cross, norm_mem, norm_ffn, xq_w, xkv_w, xo_w, xq_norm, xk_norm, ffn_in_w, ffn_conv_w, ffn_conv_b, ffn_out_w, hyb_in_w, hyb_out_w, gdn_conv_w, gdn_a_log, gdn_dt_bias, gdn_norm, lru_conv_w, lru_conv_b, lru_wr, lru_br, lru_wi, lru_bi, lru_lambda, swa_in_w, swa_out_w, swa_q_norm, swa_k_norm, swa_sinks):
    bsz, s, d = x.shape
    assert bsz == 1 and d == D_MODEL and s % 1024 == 0
    xs = x[0]
    rows3 = lambda v: v.reshape(v.shape[0], 1, v.shape[1]).astype(F32)

    norm_mix3, norm_cross3, norm_ffn3 = rows3(norm_mix), rows3(norm_cross), rows3(norm_ffn)
    ba0 = GDN_QKV + GDN_V
    lru0 = ba0 + 2 * GDN_HEADS
    n_even = hyb_in_w.shape[0]
    hyb2 = hyb_in_w.reshape(n_even * d, hyb_in_w.shape[2])
    w_gdn = hyb2[:, :ba0].astype(BF16).reshape(n_even, d, ba0)
    w_lru = hyb2[:, lru0:].astype(BF16).reshape(n_even, d, 2 * LRU_WIDTH)
    w_ba = jnp.pad(hyb2[:, ba0:lru0], ((0, 0), (0, V7X_LANES - 2 * GDN_HEADS))).astype(BF16).reshape(
        n_even, d, V7X_LANES)
    hyb_out_b = hyb_out_w.astype(BF16)
    alog3, dt3 = _lane_rows(gdn_a_log, GDN_HEADS), _lane_rows(gdn_dt_bias, GDN_HEADS)
    gdn_norm3 = rows3(gdn_norm)
    lru_cb3, lru_br3, lru_bi3, lru_lam3 = rows3(lru_conv_b), rows3(lru_br), rows3(lru_bi), rows3(lru_lambda)
    lru_wr_b, lru_wi_b = lru_wr.astype(BF16), lru_wi.astype(BF16)
    swa_in_b = jnp.concatenate(
        [_swa_pair_layout(swa_in_w[:, :, :SWA_Q], 2, SWA_HEADS),
         _swa_pair_layout(swa_in_w[:, :, SWA_Q:SWA_Q + SWA_KV], 2, SWA_KV_HEADS),
         _swa_pair_layout(swa_in_w[:, :, SWA_Q + SWA_KV:], 2, SWA_KV_HEADS)], axis=2).astype(BF16)
    swa_out_b = _swa_pair_layout(swa_out_w, 1, SWA_HEADS).astype(BF16)
    swa_qg, swa_kg = _gain_lanes(swa_q_norm), _gain_lanes(swa_k_norm)
    lanes = jnp.arange(V7X_LANES)
    seg = (lanes[:, None] // SWA_HALF == lanes[None, :] // SWA_HALF).astype(BF16)
    xq_b, xo_b = xq_w.astype(BF16), xo_w.astype(BF16)
    xq_norm3 = rows3(xq_norm)
    ffn_in_b, ffn_out_b = ffn_in_w.astype(BF16), ffn_out_w.astype(BF16)
    ffn_cb3 = rows3(ffn_conv_b)

    inv = 1.0 / (ROPE_THETA ** (jnp.arange(0, SWA_HD, 2, dtype=F32) / SWA_HD))
    inv_lanes = jnp.tile(inv, SWA_GROUPS_PER_PAIR).reshape(1, V7X_LANES)
    cos, sin = _rope_tables(positions.reshape(s, 1), inv_lanes)
    kn_all, v_all = _mem_kv(mem[0], rows3(norm_mem), xkv_w.astype(BF16), rows3(xk_norm))

    for l in range(DEPTH):
        if l % 2 == 0:
            e = l // 2
            p, ba = _hyb_in(xs, norm_mix3, w_gdn, w_lru, w_ba, gdn_conv_w, lru_conv_w, lru_cb3, l, e,
                            name=f"hyb_in_{l}")
            oa = _gdn_mixer(p, ba, alog3, dt3, gdn_norm3, e, name=f"gdn_{l}")
            ob = _lru_mixer(p, lru_wr_b, lru_br3, lru_wi_b, lru_bi3, lru_lam3, e, name=f"lru_{l}")
            mixed, w_mix, wl = [oa, ob], hyb_out_b, e
        else:
            o = l // 2
            qkv = _norm_matmul(xs, norm_mix3, swa_in_b, l, o, name=f"swa_in_{l}")
            att = _swa_mixer(qkv, swa_sinks.astype(F32), cos, sin, swa_qg, swa_kg, seg, o, name=f"swa_{l}")
            mixed, w_mix, wl = [att], swa_out_b, o
        xs = _mix_cross_block(mixed, w_mix, wl, xs, norm_cross3, xq_b, xq_norm3, kn_all, v_all, xo_b, l,
                              name=f"mix_cross_{l}")
        xs = _ffn_block(xs, norm_ffn3, ffn_in_b, ffn_conv_w, ffn_cb3, ffn_out_b, l, name=f"ffn_{l}")
    return xs[None]
```

```python
import math

import jax
import jax.numpy as jnp
from jax import lax
from jax.experimental import pallas as pl
from jax.experimental.pallas import tpu as pltpu

F32 = jnp.float32
BF16 = jnp.bfloat16

D_MODEL = 2048
DEPTH = 4
MEM_LEN = 256
EPS = 1e-6
GDN_HEADS = 8
GDN_DK = 128
GDN_DV = 128
GDN_CONV = 4
GDN_CHUNK = 64
GDN_QK = GDN_HEADS * GDN_DK
GDN_V = GDN_HEADS * GDN_DV
GDN_QKV = 2 * GDN_QK + GDN_V
LRU_WIDTH = D_MODEL // 2
LRU_BLOCKS = 8
LRU_BLOCK = LRU_WIDTH // LRU_BLOCKS
LRU_CONV = 4
LRU_C = 8.0
SWA_HEADS = 32
SWA_KV_HEADS = 8
SWA_GROUP = SWA_HEADS // SWA_KV_HEADS
SWA_HD = 64
SWA_HALF = SWA_HD // 2
SWA_WINDOW = 128
SWA_BLOCK = 128
SWA_Q = SWA_HEADS * SWA_HD
SWA_KV = SWA_KV_HEADS * SWA_HD
ROPE_THETA = 10000.0
X_HEADS = 4
X_HD = 128
X_INNER = X_HEADS * X_HD
D_FF = 5632
FFN_CONV = 3
LOG2E = math.log2(math.e)

V7X_LANES = 128
V7X_SUBLANES = 8
V7X_BF16_ROWS = 16
V7X_VMEM_BYTES = 64 * 1024 * 1024

HALO = V7X_BF16_ROWS


def _vmem_limit(nbytes):
    return int(min(max(2 * nbytes, 16 * 1024 * 1024), V7X_VMEM_BYTES - 8 * 1024 * 1024))


def _params(semantics, nbytes):
    return pltpu.CompilerParams(dimension_semantics=semantics, vmem_limit_bytes=_vmem_limit(nbytes))


def _rms(x, g):
    return x * lax.rsqrt(jnp.mean(x * x, axis=-1, keepdims=True) + EPS) * g


def _sigmoid(x):
    return 1.0 / (1.0 + jnp.exp(-x))


def _softplus(x):
    return jnp.maximum(x, 0.0) + jnp.log1p(jnp.exp(-jnp.abs(x)))


def _gelu_tanh(x):
    return 0.5 * x * (1.0 + jnp.tanh(math.sqrt(2.0 / math.pi) * (x + 0.044715 * (x * x * x))))


def _dot(a, b):
    return jnp.dot(a, b, preferred_element_type=F32)


def _dot_nt(a, b):
    return lax.dot_general(a, b, (((1,), (1,)), ((), ())), preferred_element_type=F32)


def _dot_tn(a, b):
    return lax.dot_general(a, b, (((0,), (0,)), ((), ())), preferred_element_type=F32)


def _halo_index(tm):
    per_tile = tm // HALO
    return lambda i, j: (jnp.maximum(i * per_tile - 1, 0), 0)


def _causal_conv(cw_ref, src_ref, row0, rows, width):
    win = src_ref[row0 - V7X_SUBLANES:row0 + rows, :]
    y = cw_ref[width - 1:width, :] * win[V7X_SUBLANES:, :]
    for j in range(width - 1):
        back = width - 1 - j
        y = y + cw_ref[j:j + 1, :] * pltpu.roll(win, back, 0)[V7X_SUBLANES:, :]
    return y


def _norm_matmul_kernel(x_ref, g_ref, w_ref, o_ref, h_ref):
    @pl.when(pl.program_id(1) == 0)
    def _():
        h_ref[...] = _rms(x_ref[...], g_ref[...]).astype(BF16)

    o_ref[...] = _dot(h_ref[...], w_ref[...]).astype(o_ref.dtype)


def _norm_matmul(x, g_all, w_all, l, wl, *, tm=1024, tn=1024, name):
    s, d = x.shape
    n = w_all.shape[-1]
    tm, tn = min(tm, s), min(tn, n)
    nbytes = 2 * tm * d * 4 + tm * d * 2 + 2 * d * tn * 2 + 2 * tm * tn * 4
    return pl.pallas_call(
        _norm_matmul_kernel, grid=(s // tm, n // tn), name=name,
        in_specs=[pl.BlockSpec((tm, d), lambda i, j: (i, 0)),
                  pl.BlockSpec((None, 1, d), lambda i, j: (l, 0, 0)),
                  pl.BlockSpec((None, d, tn), lambda i, j: (wl, 0, j))],
        out_specs=pl.BlockSpec((tm, tn), lambda i, j: (i, j)),
        out_shape=jax.ShapeDtypeStruct((s, n), F32),
        scratch_shapes=[pltpu.VMEM((tm, d), BF16)],
        compiler_params=_params(("parallel", "arbitrary"), nbytes),
    )(x, g_all, w_all)


HYB_TN = 1024
HYB_BLOCKS = 6
HYB_EPI_ROWS = 256


HYB_PIECE = 4 * V7X_LANES


def _hyb_in_kernel(xh_ref, x_ref, g_ref, wg_ref, wl_ref, wba_ref, gcw_ref, lcw_ref, lcb_ref,
                   p_ref, ba_ref, h_ref, *pe_refs):
    tm = x_ref.shape[0]
    j = pl.program_id(1)
    tiles = range(0, tm, HYB_EPI_ROWS)
    npieces = len(pe_refs)

    @pl.when(j == 0)
    def _():
        g = g_ref[...]
        keep = (pl.program_id(0) > 0).astype(F32)
        h_ref[0:HALO, :] = _rms(xh_ref[...] * keep, g).astype(BF16)
        hb = _rms(x_ref[...], g).astype(BF16)
        h_ref[HALO:, :] = hb
        ba_ref[...] = _dot(hb, wba_ref[...])

    def matmul(blk, piece):
        w_ref = wg_ref if blk < 4 else wl_ref
        pe_refs[piece][...] = _dot(h_ref[...], w_ref[:, piece * HYB_PIECE:(piece + 1) * HYB_PIECE])

    def epilogue(blk, piece):
        src = pe_refs[piece]
        cols = slice(piece * HYB_PIECE, (piece + 1) * HYB_PIECE)
        for r0 in tiles:
            rows = slice(r0, r0 + HYB_EPI_ROWS)
            if blk <= 2:
                y = _causal_conv(gcw_ref.at[:, cols], src, HALO + r0, HYB_EPI_ROWS, GDN_CONV)
                y = y * _sigmoid(y)
                if blk == 2:
                    p_ref[rows, cols] = y.astype(p_ref.dtype)
                    continue
                qscale = GDN_DK ** -0.5 if blk == 0 else 1.0
                for h in range(HYB_PIECE // GDN_DK):
                    yh = y[:, h * GDN_DK:(h + 1) * GDN_DK]
                    inv = lax.rsqrt(jnp.sum(yh * yh, axis=-1, keepdims=True) + EPS) * qscale
                    c0 = piece * HYB_PIECE + h * GDN_DK
                    p_ref[rows, c0:c0 + GDN_DK] = (yh * inv).astype(p_ref.dtype)
            elif blk == 3:
                z = src[HALO + r0:HALO + r0 + HYB_EPI_ROWS, :]
                p_ref[rows, cols] = (z * _sigmoid(z)).astype(p_ref.dtype)
            elif blk == 4:
                xc = _causal_conv(lcw_ref.at[:, cols], src, HALO + r0, HYB_EPI_ROWS, LRU_CONV) + lcb_ref[:, cols]
                p_ref[rows, cols] = xc.astype(p_ref.dtype)
            else:
                gate = src[HALO + r0:HALO + r0 + HYB_EPI_ROWS, :]
                p_ref[rows, cols] = _gelu_tanh(gate).astype(p_ref.dtype)

    for blk in range(HYB_BLOCKS):
        @pl.when(j == blk)
        def _(blk=blk):
            for piece in range(npieces):
                matmul(blk, piece)
            for piece in range(npieces):
                epilogue(blk, piece)


def _hyb_in(x, g_all, w_gdn, w_lru, w_ba, gdn_cw, lru_cw, lru_cb, l, e, *, tm=1024, name):
    s, d = x.shape
    tm, tn = min(tm, s), HYB_TN
    nbytes = (2 * tm * d * 4 + (tm + HALO) * d * 2 + 4 * d * tn * 2 + (tm + HALO) * tn * 4
              + 2 * tm * tn * 2 + 8 * HYB_EPI_ROWS * HYB_PIECE * 4)
    return pl.pallas_call(
        _hyb_in_kernel, grid=(s // tm, HYB_BLOCKS), name=name,
        in_specs=[pl.BlockSpec((HALO, d), _halo_index(tm)),
                  pl.BlockSpec((tm, d), lambda i, j: (i, 0)),
                  pl.BlockSpec((None, 1, d), lambda i, j: (l, 0, 0)),
                  pl.BlockSpec((None, d, tn), lambda i, j: (e, 0, jnp.minimum(j, 3))),
                  pl.BlockSpec((None, d, tn), lambda i, j: (e, 0, jnp.maximum(j - 4, 0))),
                  pl.BlockSpec((None, d, V7X_LANES), lambda i, j: (e, 0, 0)),
                  pl.BlockSpec((None, GDN_CONV, tn), lambda i, j: (e, 0, jnp.minimum(j, 2))),
                  pl.BlockSpec((None, LRU_CONV, LRU_WIDTH), lambda i, j: (e, 0, 0)),
                  pl.BlockSpec((None, 1, LRU_WIDTH), lambda i, j: (e, 0, 0))],
        out_specs=[pl.BlockSpec((tm, tn), lambda i, j: (i, j)),
                   pl.BlockSpec((tm, V7X_LANES), lambda i, j: (i, 0))],
        out_shape=[jax.ShapeDtypeStruct((s, HYB_BLOCKS * tn), BF16),
                   jax.ShapeDtypeStruct((s, V7X_LANES), F32)],
        scratch_shapes=[pltpu.VMEM((tm + HALO, d), BF16)]
        + [pltpu.VMEM((tm + HALO, HYB_PIECE), F32) for _ in range(tn // HYB_PIECE)],
        compiler_params=_params(("parallel", "arbitrary"), nbytes),
    )(x, x, g_all, w_gdn, w_lru, w_ba, gdn_cw, lru_cw, lru_cb)


def _mem_kv_kernel(mem_ref, g_ref, w_ref, kg_ref, k_ref, v_ref):
    mn = _rms(mem_ref[...], g_ref[...]).astype(BF16)
    kv = _dot(mn, w_ref[...].astype(BF16))
    kg = kg_ref[...]
    for h in range(X_HEADS):
        kh = kv[:, h * X_HD:(h + 1) * X_HD]
        k_ref[:, h * X_HD:(h + 1) * X_HD] = _rms(kh, kg).astype(BF16)
    v_ref[...] = kv[:, X_INNER:].astype(BF16)


def _mem_kv(mem, norm_mem, xkv_w, xk_norm):
    m, d = mem.shape
    nl = norm_mem.shape[0]
    out = jax.ShapeDtypeStruct((nl, m, X_INNER), BF16)
    o_spec = pl.BlockSpec((None, m, X_INNER), lambda l: (l, 0, 0))
    return pl.pallas_call(
        _mem_kv_kernel, grid=(nl,), name="mem_kv",
        in_specs=[pl.BlockSpec((m, d), lambda l: (0, 0)),
                  pl.BlockSpec((None, 1, d), lambda l: (l, 0, 0)),
                  pl.BlockSpec((None, d, 2 * X_INNER), lambda l: (l, 0, 0)),
                  pl.BlockSpec((None, 1, X_HD), lambda l: (l, 0, 0))],
        out_specs=[o_spec, o_spec], out_shape=[out, out],
        compiler_params=_params(("parallel",), 2 * d * 2 * X_INNER * 2 + m * d * 4),
    )(mem, norm_mem, xkv_w, xk_norm)


def _mix_cross_kernel(*refs):
    *a_refs, wm_ref, x_ref, g_ref, wq_ref, qg_ref, k_ref, v_ref, wo_ref, o_ref = refs
    x = x_ref[...]
    k0 = 0
    for a_ref in a_refs:
        ka = a_ref.shape[1]
        x = x + _dot(a_ref[...], wm_ref[k0:k0 + ka, :])
        k0 += ka
    h = _rms(x, g_ref[...]).astype(BF16)
    q = _dot(h, wq_ref[...].astype(BF16))
    qg = qg_ref[...]
    outs = []
    for hh in range(X_HEADS):
        sl = slice(hh * X_HD, (hh + 1) * X_HD)
        qn = (_rms(q[:, sl], qg) * (X_HD ** -0.5 * LOG2E)).astype(BF16)
        sc = _dot_nt(qn, k_ref[:, sl])
        p = jnp.exp2(sc - jnp.max(sc, axis=-1, keepdims=True))
        inv = 1.0 / jnp.sum(p, axis=-1, keepdims=True)
        outs.append((_dot(p.astype(BF16), v_ref[:, sl]) * inv).astype(BF16))
    o = jnp.concatenate(outs, axis=-1)
    o_ref[...] = x + _dot(o, wo_ref[...].astype(BF16))


def _mix_cross_block(a_list, wm_all, wl, x, g_all, wq_all, qg_all, kn_all, v_all, wo_all, l, *, tm=512, name):
    s, d = x.shape
    tm = min(tm, s)
    m = kn_all.shape[1]
    kmix = wm_all.shape[1]
    nbytes = (4 * tm * d * 4 + 2 * tm * kmix * 2 + 2 * kmix * d * 2 + 4 * d * X_INNER * 4
              + 4 * m * X_INNER * 2 + 4 * tm * m * 4)
    layer = lambda shape: pl.BlockSpec((None,) + shape, lambda i: (l, 0, 0))
    return pl.pallas_call(
        _mix_cross_kernel, grid=(s // tm,), name=name,
        in_specs=[pl.BlockSpec((tm, a.shape[1]), lambda i: (i, 0)) for a in a_list]
        + [pl.BlockSpec((None, kmix, d), lambda i: (wl, 0, 0)),
           pl.BlockSpec((tm, d), lambda i: (i, 0)), layer((1, d)), layer((d, X_INNER)),
           layer((1, X_HD)), layer((m, X_INNER)), layer((m, X_INNER)), layer((X_INNER, d))],
        out_specs=pl.BlockSpec((tm, d), lambda i: (i, 0)),
        out_shape=jax.ShapeDtypeStruct((s, d), F32),
        compiler_params=_params(("parallel",), nbytes),
    )(*a_list, wm_all, x, g_all, wq_all, qg_all, kn_all, v_all, wo_all)


def _ffn_kernel(xh_ref, x_ref, g_ref, wg_ref, wu_ref, cw_ref, cb_ref, wo_ref, o_ref, h_ref, gt_ref):
    tm = x_ref.shape[0]

    @pl.when(pl.program_id(1) == 0)
    def _():
        g = g_ref[...]
        keep = (pl.program_id(0) > 0).astype(F32)
        h_ref[0:HALO, :] = _rms(xh_ref[...] * keep, g).astype(BF16)
        x = x_ref[...]
        h_ref[HALO:, :] = _rms(x, g).astype(BF16)
        o_ref[...] = x

    gt_ref[...] = _dot(h_ref[...], wg_ref[...])
    up = _dot(h_ref[HALO:, :], wu_ref[...])
    y = _causal_conv(cw_ref, gt_ref, HALO, tm, FFN_CONV) + cb_ref[...]
    act = (y * _sigmoid(y) * up).astype(BF16)
    o_ref[...] += _dot(act, wo_ref[...])


def _ffn_block(x, g_all, w_in_all, cw_all, cb_all, w_out_all, l, *, tm=1024, tf=512, name):
    s, d = x.shape
    tm = min(tm, s)
    nf = D_FF // tf
    nbytes = 4 * tm * d * 4 + (tm + HALO) * d * 2 + 6 * d * tf * 2 + 3 * (tm + HALO) * tf * 4
    return pl.pallas_call(
        _ffn_kernel, grid=(s // tm, nf), name=name,
        in_specs=[pl.BlockSpec((HALO, d), _halo_index(tm)),
                  pl.BlockSpec((tm, d), lambda i, j: (i, 0)),
                  pl.BlockSpec((None, 1, d), lambda i, j: (l, 0, 0)),
                  pl.BlockSpec((None, d, tf), lambda i, j: (l, 0, j)),
                  pl.BlockSpec((None, d, tf), lambda i, j: (l, 0, j + nf)),
                  pl.BlockSpec((None, FFN_CONV, tf), lambda i, j: (l, 0, j)),
                  pl.BlockSpec((None, 1, tf), lambda i, j: (l, 0, j)),
                  pl.BlockSpec((None, tf, d), lambda i, j: (l, j, 0))],
        out_specs=pl.BlockSpec((tm, d), lambda i, j: (i, 0)),
        out_shape=jax.ShapeDtypeStruct((s, d), F32),
        scratch_shapes=[pltpu.VMEM((tm + HALO, d), BF16), pltpu.VMEM((tm + HALO, tf), F32)],
        compiler_params=_params(("parallel", "arbitrary"), nbytes),
    )(x, x, g_all, w_in_all, w_in_all, cw_all, cb_all, w_out_all)


GDN_STEP_CHUNKS = 4
NEUMANN_DOUBLINGS = int(math.log2(GDN_CHUNK)) - 1


def _gdn_kernel(q_ref, k_ref, v_ref, z_ref, ba_ref, alog_ref, dt_ref, ng_ref, o_ref, st_ref):
    c = GDN_CHUNK
    rows = q_ref.shape[0]
    nchunks = rows // c

    @pl.when(pl.program_id(0) == 0)
    def _():
        st_ref[...] = jnp.zeros_like(st_ref)

    ba = ba_ref[...]
    lane = lax.broadcasted_iota(jnp.int32, ba.shape, 1)
    row_in_chunk = lax.broadcasted_iota(jnp.int32, ba.shape, 0) % c
    bg = jnp.where(lane < GDN_HEADS, _sigmoid(ba), -jnp.exp(alog_ref[...]) * _softplus(ba + dt_ref[...]))
    gcs = bg
    shift = 1
    while shift < c:
        gcs = gcs + jnp.where(row_in_chunk >= shift, pltpu.roll(gcs, shift, 0), 0.0)
        shift *= 2
    egcs = jnp.exp(gcs)

    ri = lax.broadcasted_iota(jnp.int32, (c, c), 0)
    ci = lax.broadcasted_iota(jnp.int32, (c, c), 1)
    causal = ri >= ci
    strict = ri > ci
    eye = (ri == ci).astype(F32)
    ng = ng_ref[...]

    heads = range(GDN_HEADS)
    chains = [(n, h) for n in range(nchunks) for h in heads]
    rsl = lambda n: slice(n * c, (n + 1) * c)
    hsl = lambda h: slice(h * GDN_DK, (h + 1) * GDN_DK)
    gct = [jnp.concatenate([gcs[rsl(n)], jnp.zeros((c, V7X_LANES), F32)], axis=0).T for n in range(nchunks)]

    kf, beta, gc_col, egc, g_last, decay, kb = {}, {}, {}, {}, {}, {}, {}
    for n, h in chains:
        kf[n, h] = k_ref[rsl(n), hsl(h)].astype(F32)
        beta[n, h] = bg[rsl(n), h:h + 1]
        gc_col[n, h] = gcs[rsl(n), GDN_HEADS + h:GDN_HEADS + h + 1]
        egc[n, h] = egcs[rsl(n), GDN_HEADS + h:GDN_HEADS + h + 1]
        gc_row = gct[n][GDN_HEADS + h:GDN_HEADS + h + 1, 0:c]
        g_last[n, h] = gc_row[:, c - 1:c]
        decay[n, h] = jnp.where(causal, jnp.exp(jnp.minimum(gc_col[n, h] - gc_row, 0.0)), 0.0)
        kb[n, h] = kf[n, h] * beta[n, h]
    qk = {ch: _dot_nt(jnp.concatenate([q_ref[rsl(ch[0]), hsl(ch[1])], kb[ch].astype(BF16)], axis=0),
                      k_ref[rsl(ch[0]), hsl(ch[1])]) for ch in chains}
    attn = {ch: (qk[ch][0:c] * decay[ch]).astype(BF16) for ch in chains}
    lmat = {ch: jnp.where(strict, qk[ch][c:2 * c] * decay[ch], 0.0) for ch in chains}
    pm = {ch: eye - lmat[ch] for ch in chains}
    mpow = {ch: lmat[ch].astype(BF16) for ch in chains}
    for _ in range(NEUMANN_DOUBLINGS):
        mpow = {ch: _dot(mpow[ch], mpow[ch]).astype(BF16) for ch in chains}
        pm = {ch: pm[ch] + _dot(pm[ch].astype(BF16), mpow[ch]) for ch in chains}
    uw = {ch: _dot(pm[ch].astype(BF16),
                   jnp.concatenate([v_ref[rsl(ch[0]), hsl(ch[1])].astype(F32) * beta[ch], kb[ch] * egc[ch]],
                                   axis=1).astype(BF16)) for ch in chains}
    wq = {ch: jnp.concatenate([uw[ch][:, GDN_DV:],
                               q_ref[rsl(ch[0]), hsl(ch[1])].astype(F32) * egc[ch]], axis=0).astype(BF16)
          for ch in chains}
    k_dec = {ch: (kf[ch] * jnp.exp(g_last[ch] - gc_col[ch])).astype(BF16) for ch in chains}

    state = [st_ref[h] for h in heads]
    for n in range(nchunks):
        ws = [_dot(wq[n, h], state[h].astype(BF16)) for h in heads]
        v_new = [(uw[n, h][:, 0:GDN_DV] - ws[h][0:c]).astype(BF16) for h in heads]
        state = [state[h] * jnp.exp(g_last[n, h]) + _dot_tn(k_dec[n, h], v_new[h]) for h in heads]
        o = [ws[h][c:2 * c] + _dot(attn[n, h], v_new[h]) for h in heads]
        for h in heads:
            o_ref[rsl(n), hsl(h)] = (_rms(o[h], ng) * z_ref[rsl(n), hsl(h)].astype(F32)).astype(o_ref.dtype)
    for h in heads:
        st_ref[h] = state[h]


def _gdn_mixer(p, ba, alog_all, dt_all, ng_all, e, *, name):
    s = p.shape[0]
    rows = GDN_STEP_CHUNKS * GDN_CHUNK
    col = lambda b: pl.BlockSpec((rows, GDN_V), lambda i: (i, b))
    layer = lambda shape: pl.BlockSpec((None,) + shape, lambda i: (e, 0, 0))
    return pl.pallas_call(
        _gdn_kernel, grid=(s // rows,), name=name,
        in_specs=[col(0), col(1), col(2), col(3),
                  pl.BlockSpec((rows, V7X_LANES), lambda i: (i, 0)),
                  layer((1, V7X_LANES)), layer((1, V7X_LANES)), layer((1, GDN_DV))],
        out_specs=pl.BlockSpec((rows, GDN_V), lambda i: (i, 0)),
        out_shape=jax.ShapeDtypeStruct((s, GDN_V), BF16),
        scratch_shapes=[pltpu.VMEM((GDN_HEADS, GDN_DK, GDN_DV), F32)],
        compiler_params=_params(("arbitrary",), 16 * rows * GDN_V * 4),
    )(p, p, p, p, ba, alog_all, dt_all, ng_all)


def _lru_kernel(xc_ref, gate_ref, wr_ref, br_ref, wi_ref, bi_ref, lam_ref, o_ref, a_ref, b_ref, carry_ref):
    tt = xc_ref.shape[0]

    @pl.when(pl.program_id(0) == 0)
    def _():
        carry_ref[...] = jnp.zeros_like(carry_ref)

    for blk in range(LRU_BLOCKS):
        sl = slice(blk * LRU_BLOCK, (blk + 1) * LRU_BLOCK)
        xb = xc_ref[:, sl]
        r = _sigmoid(_dot(xb, wr_ref[blk]) + br_ref[:, sl])
        i = _sigmoid(_dot(xb, wi_ref[blk]) + bi_ref[:, sl])
        log_a = (-LRU_C) * r * _softplus(-lam_ref[:, sl])
        a = jnp.exp(log_a)
        one_minus_a2 = -jnp.tanh(log_a) * (1.0 + a * a)
        a_ref[:, sl] = a
        b_ref[:, sl] = jnp.sqrt(one_minus_a2) * (i * xb.astype(F32))

    row = lax.broadcasted_iota(jnp.int32, (V7X_SUBLANES, LRU_WIDTH), 0)

    def group(gi, hprev):
        r0 = pl.multiple_of(gi * V7X_SUBLANES, V7X_SUBLANES)
        a = a_ref[pl.ds(r0, V7X_SUBLANES), :]
        b = b_ref[pl.ds(r0, V7X_SUBLANES), :]
        shift = 1
        while shift < V7X_SUBLANES:
            keep = row >= shift
            b = b + a * jnp.where(keep, pltpu.roll(b, shift, 0), 0.0)
            a = a * jnp.where(keep, pltpu.roll(a, shift, 0), 1.0)
            shift *= 2
        hcur = b + a * hprev
        b_ref[pl.ds(r0, V7X_SUBLANES), :] = hcur
        return jnp.broadcast_to(hcur[V7X_SUBLANES - 1:V7X_SUBLANES, :], hcur.shape)

    carry_ref[...] = lax.fori_loop(0, tt // V7X_SUBLANES, group, carry_ref[...])
    o_ref[...] = (b_ref[...] * gate_ref[...].astype(F32)).astype(o_ref.dtype)


def _lru_mixer(p, wr_all, br_all, wi_all, bi_all, lam_all, e, *, tt=256, name):
    s = p.shape[0]
    tt = min(tt, s)
    vec = pl.BlockSpec((None, 1, LRU_WIDTH), lambda i: (e, 0, 0))
    mat = pl.BlockSpec((None, LRU_BLOCKS, LRU_BLOCK, LRU_BLOCK), lambda i: (e, 0, 0, 0))
    return pl.pallas_call(
        _lru_kernel, grid=(s // tt,), name=name,
        in_specs=[pl.BlockSpec((tt, LRU_WIDTH), lambda i: (i, 4)),
                  pl.BlockSpec((tt, LRU_WIDTH), lambda i: (i, 5)),
                  mat, vec, mat, vec, vec],
        out_specs=pl.BlockSpec((tt, LRU_WIDTH), lambda i: (i, 0)),
        out_shape=jax.ShapeDtypeStruct((s, LRU_WIDTH), BF16),
        scratch_shapes=[pltpu.VMEM((tt, LRU_WIDTH), F32),
                        pltpu.VMEM((tt, LRU_WIDTH), F32),
                        pltpu.VMEM((V7X_SUBLANES, LRU_WIDTH), F32)],
        compiler_params=_params(("arbitrary",), 10 * tt * LRU_WIDTH * 4),
    )(p, p, wr_all, br_all, wi_all, bi_all, lam_all)


def _rope_kernel(pos_ref, inv_ref, cos_ref, sin_ref):
    ang = pos_ref[...].astype(F32) * inv_ref[...]
    cos_ref[...] = jnp.cos(ang)
    sin_ref[...] = jnp.sin(ang)


def _rope_tables(pos_col, inv_lanes, *, tm=1024):
    s = pos_col.shape[0]
    tm = min(tm, s)
    out = jax.ShapeDtypeStruct((s, V7X_LANES), F32)
    o_spec = pl.BlockSpec((tm, V7X_LANES), lambda i: (i, 0))
    return pl.pallas_call(
        _rope_kernel, grid=(s // tm,), name="rope_tables",
        in_specs=[pl.BlockSpec((tm, 1), lambda i: (i, 0)), pl.BlockSpec((1, V7X_LANES), lambda i: (0, 0))],
        out_specs=[o_spec, o_spec], out_shape=[out, out],
        compiler_params=_params(("parallel",), 8 * tm * V7X_LANES * 4),
    )(pos_col, inv_lanes)


SWA_PAIR = 2 * V7X_LANES
SWA_Q_PAIRS = SWA_Q // SWA_PAIR
SWA_KV_PAIRS = SWA_KV // SWA_PAIR
SWA_GROUPS_PER_PAIR = V7X_LANES // SWA_HALF


def _swa_kernel(sink_ref, qkv_ref, cos_ref, sin_ref, qg_ref, kg_ref, seg_ref, o_ref, k_ref, v_ref):
    blk = SWA_BLOCK
    first = pl.program_id(0) == 0

    @pl.when(first)
    def _():
        k_ref[0:blk, :] = jnp.zeros((blk, SWA_KV), BF16)
        v_ref[0:blk, :] = jnp.zeros((blk, SWA_KV), BF16)

    @pl.when(jnp.logical_not(first))
    def _():
        k_ref[0:blk, :] = k_ref[blk:2 * blk, :]
        v_ref[0:blk, :] = v_ref[blk:2 * blk, :]

    cos = cos_ref[...]
    sin = sin_ref[...]
    seg = seg_ref[...]

    def norm_rope(col0, gain_ref):
        a = qkv_ref[:, col0:col0 + V7X_LANES]
        b = qkv_ref[:, col0 + V7X_LANES:col0 + SWA_PAIR]
        sq = a * a + b * b
        hi = sq.astype(BF16)
        lo = (sq - hi.astype(F32)).astype(BF16)
        ss = _dot(hi, seg) + _dot(lo, seg)
        scale = lax.rsqrt(ss * (1.0 / SWA_HD) + EPS)
        a = a * scale * gain_ref[:, 0:V7X_LANES]
        b = b * scale * gain_ref[:, V7X_LANES:SWA_PAIR]
        return a * cos - b * sin, b * cos + a * sin

    for u in range(SWA_KV_PAIRS):
        ka, kb = norm_rope(SWA_Q + u * SWA_PAIR, kg_ref)
        k_ref[blk:2 * blk, u * SWA_PAIR:u * SWA_PAIR + V7X_LANES] = ka.astype(BF16)
        k_ref[blk:2 * blk, u * SWA_PAIR + V7X_LANES:(u + 1) * SWA_PAIR] = kb.astype(BF16)
    v_ref[blk:2 * blk, :] = qkv_ref[:, SWA_Q + SWA_KV:SWA_Q + 2 * SWA_KV].astype(BF16)

    qi = lax.broadcasted_iota(jnp.int32, (blk, 2 * blk), 0)
    kj = lax.broadcasted_iota(jnp.int32, (blk, 2 * blk), 1)
    lo_key = jnp.where(first, blk, 0)
    valid = (kj > qi) & (kj <= qi + SWA_WINDOW) & (kj >= lo_key)
    lane_group = lax.broadcasted_iota(jnp.int32, (blk, SWA_PAIR), 1) % V7X_LANES // SWA_HALF

    pairs = range(SWA_Q_PAIRS)
    groups = range(SWA_GROUPS_PER_PAIR)
    kv_pair = lambda t: t // SWA_GROUPS_PER_PAIR
    kv_cols = lambda t: slice(kv_pair(t) * SWA_PAIR, (kv_pair(t) + 1) * SWA_PAIR)
    qs = []
    for t in pairs:
        qa, qb = norm_rope(t * SWA_PAIR, qg_ref)
        qp = jnp.concatenate([qa, qb], axis=1) * (SWA_HD ** -0.5 * LOG2E)
        qs.append(jnp.concatenate([jnp.where(lane_group == gl, qp, 0.0) for gl in groups],
                                  axis=0).astype(BF16))
    sc_all = [_dot_nt(qs[t], k_ref[:, kv_cols(t)]) for t in pairs]
    ps, invs = [], []
    for t in pairs:
        pt, it = [], []
        for gl in groups:
            head = SWA_GROUP * (SWA_GROUPS_PER_PAIR * kv_pair(t) + gl) + t % SWA_GROUPS_PER_PAIR
            sink = sink_ref[head] * LOG2E
            sc = jnp.where(valid, sc_all[t][gl * blk:(gl + 1) * blk], -jnp.inf)
            m = jnp.maximum(jnp.max(sc, axis=-1, keepdims=True), sink)
            p = jnp.exp2(sc - m)
            it.append(1.0 / (jnp.sum(p, axis=-1, keepdims=True) + jnp.exp2(sink - m)))
            pt.append(p.astype(BF16))
        ps.append(jnp.concatenate(pt, axis=0))
        invs.append(it)
    pv = [_dot(ps[t], v_ref[:, kv_cols(t)]) for t in pairs]
    for t in pairs:
        out = jnp.zeros((blk, SWA_PAIR), F32)
        for gl in groups:
            out = jnp.where(lane_group == gl, pv[t][gl * blk:(gl + 1) * blk] * invs[t][gl], out)
        o_ref[:, t * SWA_PAIR:(t + 1) * SWA_PAIR] = out.astype(o_ref.dtype)


def _swa_mixer(qkv, sinks_all, cos, sin, qg_all, kg_all, seg, o, *, name):
    s, n = qkv.shape
    blk = SWA_BLOCK
    layer = lambda shape: pl.BlockSpec((None,) + shape, lambda i: (o, 0, 0))
    return pl.pallas_call(
        _swa_kernel, grid=(s // blk,), name=name,
        in_specs=[pl.BlockSpec(memory_space=pltpu.SMEM),
                  pl.BlockSpec((blk, n), lambda i: (i, 0)),
                  pl.BlockSpec((blk, V7X_LANES), lambda i: (i, 0)),
                  pl.BlockSpec((blk, V7X_LANES), lambda i: (i, 0)),
                  layer((1, SWA_PAIR)), layer((1, SWA_PAIR)),
                  pl.BlockSpec((V7X_LANES, V7X_LANES), lambda i: (0, 0))],
        out_specs=pl.BlockSpec((blk, SWA_Q), lambda i: (i, 0)),
        out_shape=jax.ShapeDtypeStruct((s, SWA_Q), BF16),
        scratch_shapes=[pltpu.VMEM((2 * blk, SWA_KV), BF16), pltpu.VMEM((2 * blk, SWA_KV), BF16)],
        compiler_params=_params(("arbitrary",), 8 * blk * n * 4),
    )(sinks_all[o], qkv, cos, sin, qg_all, kg_all, seg)


def _swa_pair_layout(w, axis, heads):
    g = SWA_GROUPS_PER_PAIR
    lead, tail = w.shape[:axis], w.shape[axis + 1:]
    nl = len(lead)
    if heads == SWA_HEADS:
        w = w.reshape(lead + (heads // (g * SWA_GROUP), g, SWA_GROUP, 2, SWA_HALF) + tail)
        perm = (0, 2, 3, 1, 4)
    else:
        w = w.reshape(lead + (heads // g, g, 2, SWA_HALF) + tail)
        perm = (0, 2, 1, 3)
    k = len(perm)
    full = tuple(range(nl)) + tuple(nl + p for p in perm) + tuple(range(nl + k, nl + k + len(tail)))
    return w.transpose(full).reshape(lead + (heads * SWA_HD,) + tail)


def _gain_lanes(g):
    n = g.shape[0]
    g = jnp.broadcast_to(g.reshape(n, 2, 1, SWA_HALF), (n, 2, SWA_GROUPS_PER_PAIR, SWA_HALF))
    return g.reshape(n, 1, SWA_PAIR).astype(F32)


def _lane_rows(vals, offset):
    n, m = vals.shape
    return jnp.pad(vals.astype(F32), ((0, 0), (offset, V7X_LANES - offset - m))).reshape(n, 1, V7X_LANES)


def kernel(x, mem, positions, norm_mix, norm_cross, norm_mem, norm_ffn, xq_w, xkv_w, xo_w, xq_norm, xk_norm, ffn_in_w, ffn_conv_w, ffn_conv_b, ffn_out_w, hyb_in_w, hyb_out_w, gdn_conv_w, gdn_a_log, gdn_dt_bias, gdn_norm, lru_conv_w, lru_conv_b, lru_wr, lru_br, lru_wi, lru_bi, lru_lambda, swa_in_w, swa_out_w, swa_q_norm, swa_k_norm, swa_sinks):
    bsz, s, d = x.shape
    assert bsz == 1 and d == D_MODEL and s % 1024 == 0
    xs = x[0]
    rows3 = lambda v: v.reshape(v.shape[0], 1, v.shape[1]).astype(F32)

    norm_mix3, norm_cross3, norm_ffn3 = rows3(norm_mix), rows3(norm_cross), rows3(norm_ffn)
    ba0 = GDN_QKV + GDN_V
    lru0 = ba0 + 2 * GDN_HEADS
    n_even = hyb_in_w.shape[0]
    hyb2 = hyb_in_w.reshape(n_even * d, hyb_in_w.shape[2])
    w_gdn = hyb2[:, :ba0].astype(BF16).reshape(n_even, d, ba0)
    w_lru = hyb2[:, lru0:].astype(BF16).reshape(n_even, d, 2 * LRU_WIDTH)
    w_ba = jnp.pad(hyb2[:, ba0:lru0], ((0, 0), (0, V7X_LANES - 2 * GDN_HEADS))).astype(BF16).reshape(
        n_even, d, V7X_LANES)
    hyb_out_b = hyb_out_w.astype(BF16)
    alog3, dt3 = _lane_rows(gdn_a_log, GDN_HEADS), _lane_rows(gdn_dt_bias, GDN_HEADS)
    gdn_norm3 = rows3(gdn_norm)
    lru_cb3, lru_br3, lru_bi3, lru_lam3 = rows3(lru_conv_b), rows3(lru_br), rows3(lru_bi), rows3(lru_lambda)
    lru_wr_b, lru_wi_b = lru_wr.astype(BF16), lru_wi.astype(BF16)
    swa_in_b = jnp.concatenate(
        [_swa_pair_layout(swa_in_w[:, :, :SWA_Q], 2, SWA_HEADS),
         _swa_pair_layout(swa_in_w[:, :, SWA_Q:SWA_Q + SWA_KV], 2, SWA_KV_HEADS),
         _swa_pair_layout(swa_in_w[:, :, SWA_Q + SWA_KV:], 2, SWA_KV_HEADS)], axis=2).astype(BF16)
    swa_out_b = _swa_pair_layout(swa_out_w, 1, SWA_HEADS).astype(BF16)
    swa_qg, swa_kg = _gain_lanes(swa_q_norm), _gain_lanes(swa_k_norm)
    lanes = jnp.arange(V7X_LANES)
    seg = (lanes[:, None] // SWA_HALF == lanes[None, :] // SWA_HALF).astype(BF16)
    xq_norm3 = rows3(xq_norm)
    ffn_in_b, ffn_out_b = ffn_in_w.astype(BF16), ffn_out_w.astype(BF16)
    ffn_cb3 = rows3(ffn_conv_b)

    inv = 1.0 / (ROPE_THETA ** (jnp.arange(0, SWA_HD, 2, dtype=F32) / SWA_HD))
    inv_lanes = jnp.tile(inv, SWA_GROUPS_PER_PAIR).reshape(1, V7X_LANES)
    cos, sin = _rope_tables(positions.reshape(s, 1), inv_lanes)
    kn_all, v_all = _mem_kv(mem[0], rows3(norm_mem), xkv_w, rows3(xk_norm))

    for l in range(DEPTH):
        if l % 2 == 0:
            e = l // 2
            p, ba = _hyb_in(xs, norm_mix3, w_gdn, w_lru, w_ba, gdn_conv_w, lru_conv_w, lru_cb3, l, e,
                            name=f"hyb_in_{l}")
            oa = _gdn_mixer(p, ba, alog3, dt3, gdn_norm3, e, name=f"gdn_{l}")
            ob = _lru_mixer(p, lru_wr_b, lru_br3, lru_wi_b, lru_bi3, lru_lam3, e, name=f"lru_{l}")
            mixed, w_mix, wl = [oa, ob], hyb_out_b, e
        else:
            o = l // 2
            qkv = _norm_matmul(xs, norm_mix3, swa_in_b, l, o, name=f"swa_in_{l}")
            att = _swa_mixer(qkv, swa_sinks.astype(F32), cos, sin, swa_qg, swa_kg, seg, o, name=f"swa_{l}")
            mixed, w_mix, wl = [att], swa_out_b, o
        xs = _mix_cross_block(mixed, w_mix, wl, xs, norm_cross3, xq_w, xq_norm3, kn_all, v_all, xo_w, l,
                              name=f"mix_cross_{l}")
        xs = _ffn_block(xs, norm_ffn3, ffn_in_b, ffn_conv_w, ffn_cb3, ffn_out_b, l, name=f"ffn_{l}")
    return xs[None]
```

```python
import math

import jax
import jax.numpy as jnp
from jax import lax
from jax.experimental import pallas as pl
from jax.experimental.pallas import tpu as pltpu

F32 = jnp.float32
BF16 = jnp.bfloat16

D_MODEL = 2048
DEPTH = 4
MEM_LEN = 256
EPS = 1e-6
GDN_HEADS = 8
GDN_DK = 128
GDN_DV = 128
GDN_CONV = 4
GDN_CHUNK = 64
GDN_QK = GDN_HEADS * GDN_DK
GDN_V = GDN_HEADS * GDN_DV
GDN_QKV = 2 * GDN_QK + GDN_V
LRU_WIDTH = D_MODEL // 2
LRU_BLOCKS = 8
LRU_BLOCK = LRU_WIDTH // LRU_BLOCKS
LRU_CONV = 4
LRU_C = 8.0
SWA_HEADS = 32
SWA_KV_HEADS = 8
SWA_GROUP = SWA_HEADS // SWA_KV_HEADS
SWA_HD = 64
SWA_HALF = SWA_HD // 2
SWA_WINDOW = 128
SWA_BLOCK = 128
SWA_Q = SWA_HEADS * SWA_HD
SWA_KV = SWA_KV_HEADS * SWA_HD
ROPE_THETA = 10000.0
X_HEADS = 4
X_HD = 128
X_INNER = X_HEADS * X_HD
D_FF = 5632
FFN_CONV = 3
LOG2E = math.log2(math.e)

V7X_LANES = 128
V7X_SUBLANES = 8
V7X_BF16_ROWS = 16
V7X_VMEM_BYTES = 64 * 1024 * 1024

HALO = V7X_BF16_ROWS


def _vmem_limit(nbytes):
    return int(min(max(2 * nbytes, 16 * 1024 * 1024), V7X_VMEM_BYTES - 8 * 1024 * 1024))


def _params(semantics, nbytes):
    return pltpu.CompilerParams(dimension_semantics=semantics, vmem_limit_bytes=_vmem_limit(nbytes))


def _rms(x, g):
    return x * lax.rsqrt(jnp.mean(x * x, axis=-1, keepdims=True) + EPS) * g


def _sigmoid(x):
    return 1.0 / (1.0 + jnp.exp(-x))


def _sigmoid_tanh(x):
    return 0.5 * jnp.tanh(0.5 * x) + 0.5


def _softplus(x):
    return jnp.maximum(x, 0.0) + jnp.log1p(jnp.exp(-jnp.abs(x)))


def _gelu_tanh(x):
    return 0.5 * x * (1.0 + jnp.tanh(math.sqrt(2.0 / math.pi) * (x + 0.044715 * (x * x * x))))


def _dot(a, b):
    return jnp.dot(a, b, preferred_element_type=F32)


def _dot_nt(a, b):
    return lax.dot_general(a, b, (((1,), (1,)), ((), ())), preferred_element_type=F32)


def _dot_tn(a, b):
    return lax.dot_general(a, b, (((0,), (0,)), ((), ())), preferred_element_type=F32)


def _halo_index(tm):
    per_tile = tm // HALO
    return lambda i, j: (jnp.maximum(i * per_tile - 1, 0), 0)


def _causal_conv(cw_ref, src_ref, row0, rows, width):
    win = src_ref[row0 - V7X_SUBLANES:row0 + rows, :]
    y = cw_ref[width - 1:width, :] * win[V7X_SUBLANES:, :]
    for j in range(width - 1):
        back = width - 1 - j
        y = y + cw_ref[j:j + 1, :] * pltpu.roll(win, back, 0)[V7X_SUBLANES:, :]
    return y


def _norm_matmul_kernel(x_ref, g_ref, w_ref, o_ref, h_ref):
    @pl.when(pl.program_id(1) == 0)
    def _():
        h_ref[...] = _rms(x_ref[...], g_ref[...]).astype(BF16)

    o_ref[...] = _dot(h_ref[...], w_ref[...]).astype(o_ref.dtype)


def _norm_matmul(x, g_all, w_all, l, wl, *, tm=1024, tn=1024, name):
    s, d = x.shape
    n = w_all.shape[-1]
    tm, tn = min(tm, s), min(tn, n)
    nbytes = 2 * tm * d * 4 + tm * d * 2 + 2 * d * tn * 2 + 2 * tm * tn * 4
    return pl.pallas_call(
        _norm_matmul_kernel, grid=(s // tm, n // tn), name=name,
        in_specs=[pl.BlockSpec((tm, d), lambda i, j: (i, 0)),
                  pl.BlockSpec((None, 1, d), lambda i, j: (l, 0, 0)),
                  pl.BlockSpec((None, d, tn), lambda i, j: (wl, 0, j))],
        out_specs=pl.BlockSpec((tm, tn), lambda i, j: (i, j)),
        out_shape=jax.ShapeDtypeStruct((s, n), F32),
        scratch_shapes=[pltpu.VMEM((tm, d), BF16)],
        compiler_params=_params(("parallel", "arbitrary"), nbytes),
    )(x, g_all, w_all)


HYB_TN = 1024
HYB_BLOCKS = 6
HYB_EPI_ROWS = 256


HYB_PIECE = 4 * V7X_LANES


def _hyb_in_kernel(xh_ref, x_ref, g_ref, wg_ref, wl_ref, wba_ref, gcw_ref, lcw_ref, lcb_ref,
                   p_ref, ba_ref, h_ref, *pe_refs):
    tm = x_ref.shape[0]
    j = pl.program_id(1)
    tiles = range(0, tm, HYB_EPI_ROWS)
    npieces = len(pe_refs)

    @pl.when(j == 0)
    def _():
        g = g_ref[...]
        keep = (pl.program_id(0) > 0).astype(F32)
        h_ref[0:HALO, :] = _rms(xh_ref[...] * keep, g).astype(BF16)
        hb = _rms(x_ref[...], g).astype(BF16)
        h_ref[HALO:, :] = hb
        ba_ref[...] = _dot(hb, wba_ref[...])

    def matmul(blk, piece):
        w_ref = wg_ref if blk < 4 else wl_ref
        pe_refs[piece][...] = _dot(h_ref[...], w_ref[:, piece * HYB_PIECE:(piece + 1) * HYB_PIECE])

    def epilogue(blk, piece):
        src = pe_refs[piece]
        cols = slice(piece * HYB_PIECE, (piece + 1) * HYB_PIECE)
        for r0 in tiles:
            rows = slice(r0, r0 + HYB_EPI_ROWS)
            if blk <= 2:
                y = _causal_conv(gcw_ref.at[:, cols], src, HALO + r0, HYB_EPI_ROWS, GDN_CONV)
                y = y * _sigmoid(y)
                if blk == 2:
                    p_ref[rows, cols] = y.astype(p_ref.dtype)
                    continue
                qscale = GDN_DK ** -0.5 if blk == 0 else 1.0
                for h in range(HYB_PIECE // GDN_DK):
                    yh = y[:, h * GDN_DK:(h + 1) * GDN_DK]
                    inv = lax.rsqrt(jnp.sum(yh * yh, axis=-1, keepdims=True) + EPS) * qscale
                    c0 = piece * HYB_PIECE + h * GDN_DK
                    p_ref[rows, c0:c0 + GDN_DK] = (yh * inv).astype(p_ref.dtype)
            elif blk == 3:
                z = src[HALO + r0:HALO + r0 + HYB_EPI_ROWS, :]
                p_ref[rows, cols] = (z * _sigmoid(z)).astype(p_ref.dtype)
            elif blk == 4:
                xc = _causal_conv(lcw_ref.at[:, cols], src, HALO + r0, HYB_EPI_ROWS, LRU_CONV) + lcb_ref[:, cols]
                p_ref[rows, cols] = xc.astype(p_ref.dtype)
            else:
                gate = src[HALO + r0:HALO + r0 + HYB_EPI_ROWS, :]
                p_ref[rows, cols] = _gelu_tanh(gate).astype(p_ref.dtype)

    for blk in range(HYB_BLOCKS):
        @pl.when(j == blk)
        def _(blk=blk):
            for piece in range(npieces):
                matmul(blk, piece)
            for piece in range(npieces):
                epilogue(blk, piece)


def _hyb_in(x, g_all, w_gdn, w_lru, w_ba, gdn_cw, lru_cw, lru_cb, l, e, *, tm=1024, name):
    s, d = x.shape
    tm, tn = min(tm, s), HYB_TN
    nbytes = (2 * tm * d * 4 + (tm + HALO) * d * 2 + 4 * d * tn * 2 + (tm + HALO) * tn * 4
              + 2 * tm * tn * 2 + 8 * HYB_EPI_ROWS * HYB_PIECE * 4)
    return pl.pallas_call(
        _hyb_in_kernel, grid=(s // tm, HYB_BLOCKS), name=name,
        in_specs=[pl.BlockSpec((HALO, d), _halo_index(tm)),
                  pl.BlockSpec((tm, d), lambda i, j: (i, 0)),
                  pl.BlockSpec((None, 1, d), lambda i, j: (l, 0, 0)),
                  pl.BlockSpec((None, d, tn), lambda i, j: (e, 0, jnp.minimum(j, 3))),
                  pl.BlockSpec((None, d, tn), lambda i, j: (e, 0, jnp.maximum(j - 4, 0))),
                  pl.BlockSpec((None, d, V7X_LANES), lambda i, j: (e, 0, 0)),
                  pl.BlockSpec((None, GDN_CONV, tn), lambda i, j: (e, 0, jnp.minimum(j, 2))),
                  pl.BlockSpec((None, LRU_CONV, LRU_WIDTH), lambda i, j: (e, 0, 0)),
                  pl.BlockSpec((None, 1, LRU_WIDTH), lambda i, j: (e, 0, 0))],
        out_specs=[pl.BlockSpec((tm, tn), lambda i, j: (i, j)),
                   pl.BlockSpec((tm, V7X_LANES), lambda i, j: (i, 0))],
        out_shape=[jax.ShapeDtypeStruct((s, HYB_BLOCKS * tn), BF16),
                   jax.ShapeDtypeStruct((s, V7X_LANES), F32)],
        scratch_shapes=[pltpu.VMEM((tm + HALO, d), BF16)]
        + [pltpu.VMEM((tm + HALO, HYB_PIECE), F32) for _ in range(tn // HYB_PIECE)],
        compiler_params=_params(("parallel", "arbitrary"), nbytes),
    )(x, x, g_all, w_gdn, w_lru, w_ba, gdn_cw, lru_cw, lru_cb)


def _mem_kv_kernel(mem_ref, g_ref, w_ref, kg_ref, k_ref, v_ref):
    mn = _rms(mem_ref[...], g_ref[...]).astype(BF16)
    kv = _dot(mn, w_ref[...].astype(BF16))
    kg = kg_ref[...]
    for h in range(X_HEADS):
        kh = kv[:, h * X_HD:(h + 1) * X_HD]
        k_ref[:, h * X_HD:(h + 1) * X_HD] = _rms(kh, kg).astype(BF16)
    v_ref[...] = kv[:, X_INNER:].astype(BF16)


def _mem_kv(mem, norm_mem, xkv_w, xk_norm):
    m, d = mem.shape
    nl = norm_mem.shape[0]
    out = jax.ShapeDtypeStruct((nl, m, X_INNER), BF16)
    o_spec = pl.BlockSpec((None, m, X_INNER), lambda l: (l, 0, 0))
    return pl.pallas_call(
        _mem_kv_kernel, grid=(nl,), name="mem_kv",
        in_specs=[pl.BlockSpec((m, d), lambda l: (0, 0)),
                  pl.BlockSpec((None, 1, d), lambda l: (l, 0, 0)),
                  pl.BlockSpec((None, d, 2 * X_INNER), lambda l: (l, 0, 0)),
                  pl.BlockSpec((None, 1, X_HD), lambda l: (l, 0, 0))],
        out_specs=[o_spec, o_spec], out_shape=[out, out],
        compiler_params=_params(("parallel",), 2 * d * 2 * X_INNER * 2 + m * d * 4),
    )(mem, norm_mem, xkv_w, xk_norm)


def _mix_cross_kernel(*refs):
    *a_refs, wm_ref, x_ref, g_ref, wq_ref, qg_ref, k_ref, v_ref, wo_ref, o_ref = refs
    x = x_ref[...]
    k0 = 0
    for a_ref in a_refs:
        ka = a_ref.shape[1]
        x = x + _dot(a_ref[...], wm_ref[k0:k0 + ka, :])
        k0 += ka
    h = _rms(x, g_ref[...]).astype(BF16)
    q = _dot(h, wq_ref[...].astype(BF16))
    qg = qg_ref[...]
    outs = []
    for hh in range(X_HEADS):
        sl = slice(hh * X_HD, (hh + 1) * X_HD)
        qn = (_rms(q[:, sl], qg) * (X_HD ** -0.5 * LOG2E)).astype(BF16)
        sc = _dot_nt(qn, k_ref[:, sl])
        p = jnp.exp2(sc - jnp.max(sc, axis=-1, keepdims=True))
        inv = 1.0 / jnp.sum(p, axis=-1, keepdims=True)
        outs.append((_dot(p.astype(BF16), v_ref[:, sl]) * inv).astype(BF16))
    o = jnp.concatenate(outs, axis=-1)
    o_ref[...] = x + _dot(o, wo_ref[...].astype(BF16))


def _mix_cross_block(a_list, wm_all, wl, x, g_all, wq_all, qg_all, kn_all, v_all, wo_all, l, *, tm=512, name):
    s, d = x.shape
    tm = min(tm, s)
    m = kn_all.shape[1]
    kmix = wm_all.shape[1]
    nbytes = (4 * tm * d * 4 + 2 * tm * kmix * 2 + 2 * kmix * d * 2 + 4 * d * X_INNER * 4
              + 4 * m * X_INNER * 2 + 4 * tm * m * 4)
    layer = lambda shape: pl.BlockSpec((None,) + shape, lambda i: (l, 0, 0))
    return pl.pallas_call(
        _mix_cross_kernel, grid=(s // tm,), name=name,
        in_specs=[pl.BlockSpec((tm, a.shape[1]), lambda i: (i, 0)) for a in a_list]
        + [pl.BlockSpec((None, kmix, d), lambda i: (wl, 0, 0)),
           pl.BlockSpec((tm, d), lambda i: (i, 0)), layer((1, d)), layer((d, X_INNER)),
           layer((1, X_HD)), layer((m, X_INNER)), layer((m, X_INNER)), layer((X_INNER, d))],
        out_specs=pl.BlockSpec((tm, d), lambda i: (i, 0)),
        out_shape=jax.ShapeDtypeStruct((s, d), F32),
        compiler_params=_params(("parallel",), nbytes),
    )(*a_list, wm_all, x, g_all, wq_all, qg_all, kn_all, v_all, wo_all)


def _ffn_kernel(xh_ref, x_ref, g_ref, wg_ref, wu_ref, cw_ref, cb_ref, wo_ref, o_ref, h_ref, gt_ref):
    tm = x_ref.shape[0]

    @pl.when(pl.program_id(1) == 0)
    def _():
        g = g_ref[...]
        keep = (pl.program_id(0) > 0).astype(F32)
        h_ref[0:HALO, :] = _rms(xh_ref[...] * keep, g).astype(BF16)
        x = x_ref[...]
        h_ref[HALO:, :] = _rms(x, g).astype(BF16)
        o_ref[...] = x

    gt_ref[...] = _dot(h_ref[...], wg_ref[...])
    up = _dot(h_ref[HALO:, :], wu_ref[...])
    y = _causal_conv(cw_ref, gt_ref, HALO, tm, FFN_CONV) + cb_ref[...]
    act = (y * _sigmoid(y) * up).astype(BF16)
    o_ref[...] += _dot(act, wo_ref[...])


def _ffn_block(x, g_all, w_in_all, cw_all, cb_all, w_out_all, l, *, tm=1024, tf=512, name):
    s, d = x.shape
    tm = min(tm, s)
    nf = D_FF // tf
    nbytes = 4 * tm * d * 4 + (tm + HALO) * d * 2 + 6 * d * tf * 2 + 3 * (tm + HALO) * tf * 4
    return pl.pallas_call(
        _ffn_kernel, grid=(s // tm, nf), name=name,
        in_specs=[pl.BlockSpec((HALO, d), _halo_index(tm)),
                  pl.BlockSpec((tm, d), lambda i, j: (i, 0)),
                  pl.BlockSpec((None, 1, d), lambda i, j: (l, 0, 0)),
                  pl.BlockSpec((None, d, tf), lambda i, j: (l, 0, j)),
                  pl.BlockSpec((None, d, tf), lambda i, j: (l, 0, j + nf)),
                  pl.BlockSpec((None, FFN_CONV, tf), lambda i, j: (l, 0, j)),
                  pl.BlockSpec((None, 1, tf), lambda i, j: (l, 0, j)),
                  pl.BlockSpec((None, tf, d), lambda i, j: (l, j, 0))],
        out_specs=pl.BlockSpec((tm, d), lambda i, j: (i, 0)),
        out_shape=jax.ShapeDtypeStruct((s, d), F32),
        scratch_shapes=[pltpu.VMEM((tm + HALO, d), BF16), pltpu.VMEM((tm + HALO, tf), F32)],
        compiler_params=_params(("parallel", "arbitrary"), nbytes),
    )(x, x, g_all, w_in_all, w_in_all, cw_all, cb_all, w_out_all)


GDN_STEP_CHUNKS = 4
NEUMANN_DOUBLINGS = int(math.log2(GDN_CHUNK)) - 1


def _gdn_kernel(q_ref, k_ref, v_ref, z_ref, ba_ref, alog_ref, dt_ref, ng_ref, o_ref, st_ref):
    c = GDN_CHUNK
    rows = q_ref.shape[0]
    nchunks = rows // c

    @pl.when(pl.program_id(0) == 0)
    def _():
        st_ref[...] = jnp.zeros_like(st_ref)

    ba = ba_ref[...]
    lane = lax.broadcasted_iota(jnp.int32, ba.shape, 1)
    row_in_chunk = lax.broadcasted_iota(jnp.int32, ba.shape, 0) % c
    bg = jnp.where(lane < GDN_HEADS, _sigmoid(ba), -jnp.exp(alog_ref[...]) * _softplus(ba + dt_ref[...]))
    gcs = bg
    shift = 1
    while shift < c:
        gcs = gcs + jnp.where(row_in_chunk >= shift, pltpu.roll(gcs, shift, 0), 0.0)
        shift *= 2
    egcs = jnp.exp(gcs)

    ri = lax.broadcasted_iota(jnp.int32, (c, c), 0)
    ci = lax.broadcasted_iota(jnp.int32, (c, c), 1)
    causal = ri >= ci
    strict = ri > ci
    eye = (ri == ci).astype(F32)
    ng = ng_ref[...]

    heads = range(GDN_HEADS)
    chains = [(n, h) for n in range(nchunks) for h in heads]
    rsl = lambda n: slice(n * c, (n + 1) * c)
    hsl = lambda h: slice(h * GDN_DK, (h + 1) * GDN_DK)
    gct = [jnp.concatenate([gcs[rsl(n)], jnp.zeros((c, V7X_LANES), F32)], axis=0).T for n in range(nchunks)]

    kf, beta, gc_col, egc, g_last, decay, kb = {}, {}, {}, {}, {}, {}, {}
    for n, h in chains:
        kf[n, h] = k_ref[rsl(n), hsl(h)].astype(F32)
        beta[n, h] = bg[rsl(n), h:h + 1]
        gc_col[n, h] = gcs[rsl(n), GDN_HEADS + h:GDN_HEADS + h + 1]
        egc[n, h] = egcs[rsl(n), GDN_HEADS + h:GDN_HEADS + h + 1]
        gc_row = gct[n][GDN_HEADS + h:GDN_HEADS + h + 1, 0:c]
        g_last[n, h] = gc_row[:, c - 1:c]
        decay[n, h] = jnp.where(causal, jnp.exp(jnp.minimum(gc_col[n, h] - gc_row, 0.0)), 0.0)
        kb[n, h] = kf[n, h] * beta[n, h]
    qk = {ch: _dot_nt(jnp.concatenate([q_ref[rsl(ch[0]), hsl(ch[1])], kb[ch].astype(BF16)], axis=0),
                      k_ref[rsl(ch[0]), hsl(ch[1])]) for ch in chains}
    attn = {ch: (qk[ch][0:c] * decay[ch]).astype(BF16) for ch in chains}
    lmat = {ch: jnp.where(strict, qk[ch][c:2 * c] * decay[ch], 0.0) for ch in chains}
    pm = {ch: eye - lmat[ch] for ch in chains}
    mpow = {ch: lmat[ch].astype(BF16) for ch in chains}
    for _ in range(NEUMANN_DOUBLINGS):
        mpow = {ch: _dot(mpow[ch], mpow[ch]).astype(BF16) for ch in chains}
        pm = {ch: pm[ch] + _dot(pm[ch].astype(BF16), mpow[ch]) for ch in chains}
    uw = {ch: _dot(pm[ch].astype(BF16),
                   jnp.concatenate([v_ref[rsl(ch[0]), hsl(ch[1])].astype(F32) * beta[ch], kb[ch] * egc[ch]],
                                   axis=1).astype(BF16)) for ch in chains}
    wq = {ch: jnp.concatenate([uw[ch][:, GDN_DV:],
                               q_ref[rsl(ch[0]), hsl(ch[1])].astype(F32) * egc[ch]], axis=0).astype(BF16)
          for ch in chains}
    k_dec = {ch: (kf[ch] * jnp.exp(g_last[ch] - gc_col[ch])).astype(BF16) for ch in chains}

    state = [st_ref[h] for h in heads]
    for n in range(nchunks):
        ws = [_dot(wq[n, h], state[h].astype(BF16)) for h in heads]
        v_new = [(uw[n, h][:, 0:GDN_DV] - ws[h][0:c]).astype(BF16) for h in heads]
        state = [state[h] * jnp.exp(g_last[n, h]) + _dot_tn(k_dec[n, h], v_new[h]) for h in heads]
        o = [ws[h][c:2 * c] + _dot(attn[n, h], v_new[h]) for h in heads]
        for h in heads:
            o_ref[rsl(n), hsl(h)] = (_rms(o[h], ng) * z_ref[rsl(n), hsl(h)].astype(F32)).astype(o_ref.dtype)
    for h in heads:
        st_ref[h] = state[h]


def _gdn_mixer(p, ba, alog_all, dt_all, ng_all, e, *, name):
    s = p.shape[0]
    rows = GDN_STEP_CHUNKS * GDN_CHUNK
    col = lambda b: pl.BlockSpec((rows, GDN_V), lambda i: (i, b))
    layer = lambda shape: pl.BlockSpec((None,) + shape, lambda i: (e, 0, 0))
    return pl.pallas_call(
        _gdn_kernel, grid=(s // rows,), name=name,
        in_specs=[col(0), col(1), col(2), col(3),
                  pl.BlockSpec((rows, V7X_LANES), lambda i: (i, 0)),
                  layer((1, V7X_LANES)), layer((1, V7X_LANES)), layer((1, GDN_DV))],
        out_specs=pl.BlockSpec((rows, GDN_V), lambda i: (i, 0)),
        out_shape=jax.ShapeDtypeStruct((s, GDN_V), BF16),
        scratch_shapes=[pltpu.VMEM((GDN_HEADS, GDN_DK, GDN_DV), F32)],
        compiler_params=_params(("arbitrary",), 16 * rows * GDN_V * 4),
    )(p, p, p, p, ba, alog_all, dt_all, ng_all)


def _lru_kernel(xc_ref, gate_ref, wr_ref, br_ref, wi_ref, bi_ref, lam_ref, o_ref, a_ref, b_ref, carry_ref):
    tt = xc_ref.shape[0]

    @pl.when(pl.program_id(0) == 0)
    def _():
        carry_ref[...] = jnp.zeros_like(carry_ref)

    for blk in range(LRU_BLOCKS):
        sl = slice(blk * LRU_BLOCK, (blk + 1) * LRU_BLOCK)
        xb = xc_ref[:, sl]
        r = _sigmoid_tanh(_dot(xb, wr_ref[blk]) + br_ref[:, sl])
        i = _sigmoid_tanh(_dot(xb, wi_ref[blk]) + bi_ref[:, sl])
        log_a = (-LRU_C) * r * _softplus(-lam_ref[:, sl])
        a = jnp.exp(log_a)
        one_minus_a2 = -jnp.tanh(log_a) * (1.0 + a * a)
        a_ref[:, sl] = a
        b_ref[:, sl] = jnp.sqrt(one_minus_a2) * (i * xb.astype(F32))

    row = lax.broadcasted_iota(jnp.int32, (V7X_SUBLANES, LRU_WIDTH), 0)

    def group(gi, hprev):
        r0 = pl.multiple_of(gi * V7X_SUBLANES, V7X_SUBLANES)
        a = a_ref[pl.ds(r0, V7X_SUBLANES), :]
        b = b_ref[pl.ds(r0, V7X_SUBLANES), :]
        shift = 1
        while shift < V7X_SUBLANES:
            keep = row >= shift
            b = b + a * jnp.where(keep, pltpu.roll(b, shift, 0), 0.0)
            a = a * jnp.where(keep, pltpu.roll(a, shift, 0), 1.0)
            shift *= 2
        hcur = b + a * hprev
        b_ref[pl.ds(r0, V7X_SUBLANES), :] = hcur
        return jnp.broadcast_to(hcur[V7X_SUBLANES - 1:V7X_SUBLANES, :], hcur.shape)

    carry_ref[...] = lax.fori_loop(0, tt // V7X_SUBLANES, group, carry_ref[...])
    o_ref[...] = (b_ref[...] * gate_ref[...].astype(F32)).astype(o_ref.dtype)


def _lru_mixer(p, wr_all, br_all, wi_all, bi_all, lam_all, e, *, tt=256, name):
    s = p.shape[0]
    tt = min(tt, s)
    vec = pl.BlockSpec((None, 1, LRU_WIDTH), lambda i: (e, 0, 0))
    mat = pl.BlockSpec((None, LRU_BLOCKS, LRU_BLOCK, LRU_BLOCK), lambda i: (e, 0, 0, 0))
    return pl.pallas_call(
        _lru_kernel, grid=(s // tt,), name=name,
        in_specs=[pl.BlockSpec((tt, LRU_WIDTH), lambda i: (i, 4)),
                  pl.BlockSpec((tt, LRU_WIDTH), lambda i: (i, 5)),
                  mat, vec, mat, vec, vec],
        out_specs=pl.BlockSpec((tt, LRU_WIDTH), lambda i: (i, 0)),
        out_shape=jax.ShapeDtypeStruct((s, LRU_WIDTH), BF16),
        scratch_shapes=[pltpu.VMEM((tt, LRU_WIDTH), F32),
                        pltpu.VMEM((tt, LRU_WIDTH), F32),
                        pltpu.VMEM((V7X_SUBLANES, LRU_WIDTH), F32)],
        compiler_params=_params(("arbitrary",), 10 * tt * LRU_WIDTH * 4),
    )(p, p, wr_all, br_all, wi_all, bi_all, lam_all)


def _rope_kernel(pos_ref, inv_ref, cos_ref, sin_ref):
    ang = pos_ref[...].astype(F32) * inv_ref[...]
    cos_ref[...] = jnp.cos(ang)
    sin_ref[...] = jnp.sin(ang)


def _rope_tables(pos_col, inv_lanes, *, tm=1024):
    s = pos_col.shape[0]
    tm = min(tm, s)
    out = jax.ShapeDtypeStruct((s, V7X_LANES), F32)
    o_spec = pl.BlockSpec((tm, V7X_LANES), lambda i: (i, 0))
    return pl.pallas_call(
        _rope_kernel, grid=(s // tm,), name="rope_tables",
        in_specs=[pl.BlockSpec((tm, 1), lambda i: (i, 0)), pl.BlockSpec((1, V7X_LANES), lambda i: (0, 0))],
        out_specs=[o_spec, o_spec], out_shape=[out, out],
        compiler_params=_params(("parallel",), 8 * tm * V7X_LANES * 4),
    )(pos_col, inv_lanes)


SWA_PAIR = 2 * V7X_LANES
SWA_Q_PAIRS = SWA_Q // SWA_PAIR
SWA_KV_PAIRS = SWA_KV // SWA_PAIR
SWA_GROUPS_PER_PAIR = V7X_LANES // SWA_HALF


def _swa_kernel(sink_ref, qkv_ref, cos_ref, sin_ref, qg_ref, kg_ref, seg_ref, o_ref, k_ref, v_ref):
    blk = SWA_BLOCK
    first = pl.program_id(0) == 0

    @pl.when(first)
    def _():
        k_ref[0:blk, :] = jnp.zeros((blk, SWA_KV), BF16)
        v_ref[0:blk, :] = jnp.zeros((blk, SWA_KV), BF16)

    @pl.when(jnp.logical_not(first))
    def _():
        k_ref[0:blk, :] = k_ref[blk:2 * blk, :]
        v_ref[0:blk, :] = v_ref[blk:2 * blk, :]

    cos = cos_ref[...]
    sin = sin_ref[...]
    seg = seg_ref[...]

    def norm_rope(col0, gain_ref):
        a = qkv_ref[:, col0:col0 + V7X_LANES]
        b = qkv_ref[:, col0 + V7X_LANES:col0 + SWA_PAIR]
        sq = a * a + b * b
        hi = sq.astype(BF16)
        lo = (sq - hi.astype(F32)).astype(BF16)
        ss = _dot(hi, seg) + _dot(lo, seg)
        scale = lax.rsqrt(ss * (1.0 / SWA_HD) + EPS)
        a = a * scale * gain_ref[:, 0:V7X_LANES]
        b = b * scale * gain_ref[:, V7X_LANES:SWA_PAIR]
        return a * cos - b * sin, b * cos + a * sin

    for u in range(SWA_KV_PAIRS):
        ka, kb = norm_rope(SWA_Q + u * SWA_PAIR, kg_ref)
        k_ref[blk:2 * blk, u * SWA_PAIR:u * SWA_PAIR + V7X_LANES] = ka.astype(BF16)
        k_ref[blk:2 * blk, u * SWA_PAIR + V7X_LANES:(u + 1) * SWA_PAIR] = kb.astype(BF16)
    v_ref[blk:2 * blk, :] = qkv_ref[:, SWA_Q + SWA_KV:SWA_Q + 2 * SWA_KV].astype(BF16)

    qi = lax.broadcasted_iota(jnp.int32, (blk, 2 * blk), 0)
    kj = lax.broadcasted_iota(jnp.int32, (blk, 2 * blk), 1)
    lo_key = jnp.where(first, blk, 0)
    valid = (kj > qi) & (kj <= qi + SWA_WINDOW) & (kj >= lo_key)
    lane_group = lax.broadcasted_iota(jnp.int32, (blk, SWA_PAIR), 1) % V7X_LANES // SWA_HALF

    pairs = range(SWA_Q_PAIRS)
    groups = range(SWA_GROUPS_PER_PAIR)
    kv_pair = lambda t: t // SWA_GROUPS_PER_PAIR
    kv_cols = lambda t: slice(kv_pair(t) * SWA_PAIR, (kv_pair(t) + 1) * SWA_PAIR)
    qs = []
    for t in pairs:
        qa, qb = norm_rope(t * SWA_PAIR, qg_ref)
        qp = jnp.concatenate([qa, qb], axis=1) * (SWA_HD ** -0.5 * LOG2E)
        qs.append(jnp.concatenate([jnp.where(lane_group == gl, qp, 0.0) for gl in groups],
                                  axis=0).astype(BF16))
    sc_all = [_dot_nt(qs[t], k_ref[:, kv_cols(t)]) for t in pairs]
    ps, invs = [], []
    for t in pairs:
        pt, it = [], []
        for gl in groups:
            head = SWA_GROUP * (SWA_GROUPS_PER_PAIR * kv_pair(t) + gl) + t % SWA_GROUPS_PER_PAIR
            sink = sink_ref[head] * LOG2E
            sc = jnp.where(valid, sc_all[t][gl * blk:(gl + 1) * blk], -jnp.inf)
            m = jnp.maximum(jnp.max(sc, axis=-1, keepdims=True), sink)
            p = jnp.exp2(sc - m)
            it.append(1.0 / (jnp.sum(p, axis=-1, keepdims=True) + jnp.exp2(sink - m)))
            pt.append(p.astype(BF16))
        ps.append(jnp.concatenate(pt, axis=0))
        invs.append(it)
    pv = [_dot(ps[t], v_ref[:, kv_cols(t)]) for t in pairs]
    for t in pairs:
        out = jnp.zeros((blk, SWA_PAIR), F32)
        for gl in groups:
            out = jnp.where(lane_group == gl, pv[t][gl * blk:(gl + 1) * blk] * invs[t][gl], out)
        o_ref[:, t * SWA_PAIR:(t + 1) * SWA_PAIR] = out.astype(o_ref.dtype)


def _swa_mixer(qkv, sinks_all, cos, sin, qg_all, kg_all, seg, o, *, name):
    s, n = qkv.shape
    blk = SWA_BLOCK
    layer = lambda shape: pl.BlockSpec((None,) + shape, lambda i: (o, 0, 0))
    return pl.pallas_call(
        _swa_kernel, grid=(s // blk,), name=name,
        in_specs=[pl.BlockSpec(memory_space=pltpu.SMEM),
                  pl.BlockSpec((blk, n), lambda i: (i, 0)),
                  pl.BlockSpec((blk, V7X_LANES), lambda i: (i, 0)),
                  pl.BlockSpec((blk, V7X_LANES), lambda i: (i, 0)),
                  layer((1, SWA_PAIR)), layer((1, SWA_PAIR)),
                  pl.BlockSpec((V7X_LANES, V7X_LANES), lambda i: (0, 0))],
        out_specs=pl.BlockSpec((blk, SWA_Q), lambda i: (i, 0)),
        out_shape=jax.ShapeDtypeStruct((s, SWA_Q), BF16),
        scratch_shapes=[pltpu.VMEM((2 * blk, SWA_KV), BF16), pltpu.VMEM((2 * blk, SWA_KV), BF16)],
        compiler_params=_params(("arbitrary",), 8 * blk * n * 4),
    )(sinks_all[o], qkv, cos, sin, qg_all, kg_all, seg)


def _swa_q_src(t, half, gl):
    head = SWA_GROUPS_PER_PAIR * SWA_GROUP * (t // SWA_GROUPS_PER_PAIR) + SWA_GROUP * gl + t % SWA_GROUPS_PER_PAIR
    return head * SWA_HD + half * SWA_HALF


def _swa_kv_src(u, half, gl):
    return (SWA_GROUPS_PER_PAIR * u + gl) * SWA_HD + half * SWA_HALF


def _swa_in_prep_kernel(w_ref, o_ref):
    groups = range(SWA_GROUPS_PER_PAIR)

    def tile(srcs):
        return jnp.concatenate([w_ref[:, s:s + SWA_HALF] for s in srcs], axis=1).astype(BF16)

    for t in range(SWA_Q_PAIRS):
        for half in range(2):
            c0 = t * SWA_PAIR + half * V7X_LANES
            o_ref[:, c0:c0 + V7X_LANES] = tile([_swa_q_src(t, half, gl) for gl in groups])
    for base in (SWA_Q, SWA_Q + SWA_KV):
        for u in range(SWA_KV_PAIRS):
            for half in range(2):
                c0 = base + u * SWA_PAIR + half * V7X_LANES
                o_ref[:, c0:c0 + V7X_LANES] = tile([base + _swa_kv_src(u, half, gl) for gl in groups])


def _swa_in_prep(w, *, tr=256):
    n_layers, d, n = w.shape
    spec = pl.BlockSpec((None, tr, n), lambda e, i: (e, i, 0))
    return pl.pallas_call(
        _swa_in_prep_kernel, grid=(n_layers, d // tr), name="swa_in_prep",
        in_specs=[spec], out_specs=spec, out_shape=jax.ShapeDtypeStruct(w.shape, BF16),
        compiler_params=_params(("parallel", "parallel"), 2 * tr * n * 6),
    )(w)


def _swa_out_prep_kernel(w_ref, o_ref):
    for t in range(SWA_Q_PAIRS):
        for half in range(2):
            for gl in range(SWA_GROUPS_PER_PAIR):
                new = t * SWA_PAIR + half * V7X_LANES + gl * SWA_HALF
                src = _swa_q_src(t, half, gl)
                o_ref[new:new + SWA_HALF, :] = w_ref[src:src + SWA_HALF, :].astype(BF16)


def _swa_out_prep(w, *, tc=1024):
    n_layers, k, d = w.shape
    spec = pl.BlockSpec((None, k, tc), lambda e, j: (e, 0, j))
    return pl.pallas_call(
        _swa_out_prep_kernel, grid=(n_layers, d // tc), name="swa_out_prep",
        in_specs=[spec], out_specs=spec, out_shape=jax.ShapeDtypeStruct(w.shape, BF16),
        compiler_params=_params(("parallel", "parallel"), 2 * k * tc * 6),
    )(w)


def _hyb_prep_kernel(w_ref, gdn_ref, lru_ref, ba_ref):
    ba0 = GDN_QKV + GDN_V
    nba = 2 * GDN_HEADS
    gdn_ref[...] = w_ref[:, 0:ba0].astype(BF16)
    tail = w_ref[:, ba0:]
    lru_ref[...] = tail[:, nba:].astype(BF16)
    lane = lax.broadcasted_iota(jnp.int32, (w_ref.shape[0], V7X_LANES), 1)
    ba_ref[...] = jnp.where(lane < nba, tail[:, 0:V7X_LANES], 0.0).astype(BF16)


def _hyb_prep(w, *, tr=256):
    n_layers, d, n = w.shape
    ba0 = GDN_QKV + GDN_V
    widths = (ba0, 2 * LRU_WIDTH, V7X_LANES)
    return pl.pallas_call(
        _hyb_prep_kernel, grid=(n_layers, d // tr), name="hyb_prep",
        in_specs=[pl.BlockSpec((None, tr, n), lambda e, i: (e, i, 0))],
        out_specs=[pl.BlockSpec((None, tr, wd), lambda e, i: (e, i, 0)) for wd in widths],
        out_shape=[jax.ShapeDtypeStruct((n_layers, d, wd), BF16) for wd in widths],
        compiler_params=_params(("parallel", "parallel"), 2 * tr * n * 6),
    )(w)


def _gain_lanes(g):
    n = g.shape[0]
    g = jnp.broadcast_to(g.reshape(n, 2, 1, SWA_HALF), (n, 2, SWA_GROUPS_PER_PAIR, SWA_HALF))
    return g.reshape(n, 1, SWA_PAIR).astype(F32)


def _lane_rows(vals, offset):
    n, m = vals.shape
    return jnp.pad(vals.astype(F32), ((0, 0), (offset, V7X_LANES - offset - m))).reshape(n, 1, V7X_LANES)


def kernel(x, mem, positions, norm_mix, norm_cross, norm_mem, norm_ffn, xq_w, xkv_w, xo_w, xq_norm, xk_norm, ffn_in_w, ffn_conv_w, ffn_conv_b, ffn_out_w, hyb_in_w, hyb_out_w, gdn_conv_w, gdn_a_log, gdn_dt_bias, gdn_norm, lru_conv_w, lru_conv_b, lru_wr, lru_br, lru_wi, lru_bi, lru_lambda, swa_in_w, swa_out_w, swa_q_norm, swa_k_norm, swa_sinks):
    bsz, s, d = x.shape
    assert bsz == 1 and d == D_MODEL and s % 1024 == 0
    xs = x[0]
    rows3 = lambda v: v.reshape(v.shape[0], 1, v.shape[1]).astype(F32)

    norm_mix3, norm_cross3, norm_ffn3 = rows3(norm_mix), rows3(norm_cross), rows3(norm_ffn)
    w_gdn, w_lru, w_ba = _hyb_prep(hyb_in_w)
    hyb_out_b = hyb_out_w.astype(BF16)
    alog3, dt3 = _lane_rows(gdn_a_log, GDN_HEADS), _lane_rows(gdn_dt_bias, GDN_HEADS)
    gdn_norm3 = rows3(gdn_norm)
    lru_cb3, lru_br3, lru_bi3, lru_lam3 = rows3(lru_conv_b), rows3(lru_br), rows3(lru_bi), rows3(lru_lambda)
    lru_wr_b, lru_wi_b = lru_wr.astype(BF16), lru_wi.astype(BF16)
    swa_in_b = _swa_in_prep(swa_in_w)
    swa_out_b = _swa_out_prep(swa_out_w)
    swa_qg, swa_kg = _gain_lanes(swa_q_norm), _gain_lanes(swa_k_norm)
    lanes = jnp.arange(V7X_LANES)
    seg = (lanes[:, None] // SWA_HALF == lanes[None, :] // SWA_HALF).astype(BF16)
    xq_norm3 = rows3(xq_norm)
    ffn_in_b, ffn_out_b = ffn_in_w.astype(BF16), ffn_out_w.astype(BF16)
    ffn_cb3 = rows3(ffn_conv_b)

    inv = 1.0 / (ROPE_THETA ** (jnp.arange(0, SWA_HD, 2, dtype=F32) / SWA_HD))
    inv_lanes = jnp.tile(inv, SWA_GROUPS_PER_PAIR).reshape(1, V7X_LANES)
    cos, sin = _rope_tables(positions.reshape(s, 1), inv_lanes)
    kn_all, v_all = _mem_kv(mem[0], rows3(norm_mem), xkv_w, rows3(xk_norm))

    for l in range(DEPTH):
        if l % 2 == 0:
            e = l // 2
            p, ba = _hyb_in(xs, norm_mix3, w_gdn, w_lru, w_ba, gdn_conv_w, lru_conv_w, lru_cb3, l, e,
                            name=f"hyb_in_{l}")
            oa = _gdn_mixer(p, ba, alog3, dt3, gdn_norm3, e, name=f"gdn_{l}")
            ob = _lru_mixer(p, lru_wr_b, lru_br3, lru_wi_b, lru_bi3, lru_lam3, e, name=f"lru_{l}")
            mixed, w_mix, wl = [oa, ob], hyb_out_b, e
        else:
            o = l // 2
            qkv = _norm_matmul(xs, norm_mix3, swa_in_b, l, o, name=f"swa_in_{l}")
            att = _swa_mixer(qkv, swa_sinks.astype(F32), cos, sin, swa_qg, swa_kg, seg, o, name=f"swa_{l}")
            mixed, w_mix, wl = [att], swa_out_b, o
        xs = _mix_cross_block(mixed, w_mix, wl, xs, norm_cross3, xq_w, xq_norm3, kn_all, v_all, xo_w, l,
                              name=f"mix_cross_{l}")
        xs = _ffn_block(xs, norm_ffn3, ffn_in_b, ffn_conv_w, ffn_cb3, ffn_out_b, l, name=f"ffn_{l}")
    return xs[None]
```

```python
import math

import jax
import jax.numpy as jnp
from jax import lax
from jax.experimental import pallas as pl
from jax.experimental.pallas import tpu as pltpu

F32 = jnp.float32
BF16 = jnp.bfloat16

D_MODEL = 2048
DEPTH = 4
MEM_LEN = 256
EPS = 1e-6
GDN_HEADS = 8
GDN_DK = 128
GDN_DV = 128
GDN_CONV = 4
GDN_CHUNK = 64
GDN_QK = GDN_HEADS * GDN_DK
GDN_V = GDN_HEADS * GDN_DV
GDN_QKV = 2 * GDN_QK + GDN_V
LRU_WIDTH = D_MODEL // 2
LRU_BLOCKS = 8
LRU_BLOCK = LRU_WIDTH // LRU_BLOCKS
LRU_CONV = 4
LRU_C = 8.0
SWA_HEADS = 32
SWA_KV_HEADS = 8
SWA_GROUP = SWA_HEADS // SWA_KV_HEADS
SWA_HD = 64
SWA_HALF = SWA_HD // 2
SWA_WINDOW = 128
SWA_BLOCK = 128
SWA_Q = SWA_HEADS * SWA_HD
SWA_KV = SWA_KV_HEADS * SWA_HD
ROPE_THETA = 10000.0
X_HEADS = 4
X_HD = 128
X_INNER = X_HEADS * X_HD
D_FF = 5632
FFN_CONV = 3
LOG2E = math.log2(math.e)

V7X_LANES = 128
V7X_SUBLANES = 8
V7X_BF16_ROWS = 16
V7X_VMEM_BYTES = 64 * 1024 * 1024

HALO = V7X_BF16_ROWS


def _vmem_limit(nbytes):
    return int(min(max(2 * nbytes, 16 * 1024 * 1024), V7X_VMEM_BYTES - 8 * 1024 * 1024))


def _params(semantics, nbytes):
    return pltpu.CompilerParams(dimension_semantics=semantics, vmem_limit_bytes=_vmem_limit(nbytes))


def _rms(x, g):
    return x * lax.rsqrt(jnp.mean(x * x, axis=-1, keepdims=True) + EPS) * g


def _sigmoid(x):
    return 1.0 / (1.0 + jnp.exp(-x))


def _sigmoid_tanh(x):
    return 0.5 * jnp.tanh(0.5 * x) + 0.5


def _softplus(x):
    return jnp.maximum(x, 0.0) + jnp.log1p(jnp.exp(-jnp.abs(x)))


def _gelu_tanh(x):
    return 0.5 * x * (1.0 + jnp.tanh(math.sqrt(2.0 / math.pi) * (x + 0.044715 * (x * x * x))))


def _dot(a, b):
    return jnp.dot(a, b, preferred_element_type=F32)


def _dot_nt(a, b):
    return lax.dot_general(a, b, (((1,), (1,)), ((), ())), preferred_element_type=F32)


def _dot_tn(a, b):
    return lax.dot_general(a, b, (((0,), (0,)), ((), ())), preferred_element_type=F32)


def _halo_index(tm):
    per_tile = tm // HALO
    return lambda i, j: (jnp.maximum(i * per_tile - 1, 0), 0)


def _causal_conv(cw_ref, src_ref, row0, rows, width):
    win = src_ref[row0 - V7X_SUBLANES:row0 + rows, :]
    y = cw_ref[width - 1:width, :] * win[V7X_SUBLANES:, :]
    for j in range(width - 1):
        back = width - 1 - j
        y = y + cw_ref[j:j + 1, :] * pltpu.roll(win, back, 0)[V7X_SUBLANES:, :]
    return y


def _norm_matmul_kernel(x_ref, g_ref, w_ref, o_ref, h_ref):
    @pl.when(pl.program_id(1) == 0)
    def _():
        h_ref[...] = _rms(x_ref[...], g_ref[...]).astype(BF16)

    o_ref[...] = _dot(h_ref[...], w_ref[...]).astype(o_ref.dtype)


def _norm_matmul(x, g_all, w_all, l, wl, *, tm=1024, tn=1024, name):
    s, d = x.shape
    n = w_all.shape[-1]
    tm, tn = min(tm, s), min(tn, n)
    nbytes = 2 * tm * d * 4 + tm * d * 2 + 2 * d * tn * 2 + 2 * tm * tn * 4
    return pl.pallas_call(
        _norm_matmul_kernel, grid=(s // tm, n // tn), name=name,
        in_specs=[pl.BlockSpec((tm, d), lambda i, j: (i, 0)),
                  pl.BlockSpec((None, 1, d), lambda i, j: (l, 0, 0)),
                  pl.BlockSpec((None, d, tn), lambda i, j: (wl, 0, j))],
        out_specs=pl.BlockSpec((tm, tn), lambda i, j: (i, j)),
        out_shape=jax.ShapeDtypeStruct((s, n), F32),
        scratch_shapes=[pltpu.VMEM((tm, d), BF16)],
        compiler_params=_params(("parallel", "arbitrary"), nbytes),
    )(x, g_all, w_all)


HYB_TN = 1024
HYB_BLOCKS = 6
HYB_BA0 = GDN_QKV + GDN_V
HYB_LRU0 = HYB_BA0 + 2 * GDN_HEADS
HYB_EPI_ROWS = 256


HYB_PIECE = 4 * V7X_LANES


def _hyb_in_kernel(xh_ref, x_ref, g_ref, wg_ref, wl_ref, wba_ref, gcw_ref, lcw_ref, lcb_ref,
                   p_ref, ba_ref, h_ref, *pe_refs):
    tm = x_ref.shape[0]
    j = pl.program_id(1)
    tiles = range(0, tm, HYB_EPI_ROWS)
    npieces = len(pe_refs)

    @pl.when(j == 0)
    def _():
        g = g_ref[...]
        keep = (pl.program_id(0) > 0).astype(F32)
        h_ref[0:HALO, :] = _rms(xh_ref[...] * keep, g).astype(BF16)
        hb = _rms(x_ref[...], g).astype(BF16)
        h_ref[HALO:, :] = hb
        ba_ref[...] = _dot_nt(hb, wba_ref[...])

    def matmul(blk, piece):
        w_ref = wg_ref if blk < 4 else wl_ref.at[0]
        pe_refs[piece][...] = _dot_nt(h_ref[...], w_ref[piece * HYB_PIECE:(piece + 1) * HYB_PIECE, :])

    def epilogue(blk, piece):
        src = pe_refs[piece]
        cols = slice(piece * HYB_PIECE, (piece + 1) * HYB_PIECE)
        for r0 in tiles:
            rows = slice(r0, r0 + HYB_EPI_ROWS)
            if blk <= 2:
                y = _causal_conv(gcw_ref.at[:, cols], src, HALO + r0, HYB_EPI_ROWS, GDN_CONV)
                y = y * _sigmoid(y)
                if blk == 2:
                    p_ref[rows, cols] = y.astype(p_ref.dtype)
                    continue
                qscale = GDN_DK ** -0.5 if blk == 0 else 1.0
                for h in range(HYB_PIECE // GDN_DK):
                    yh = y[:, h * GDN_DK:(h + 1) * GDN_DK]
                    inv = lax.rsqrt(jnp.sum(yh * yh, axis=-1, keepdims=True) + EPS) * qscale
                    c0 = piece * HYB_PIECE + h * GDN_DK
                    p_ref[rows, c0:c0 + GDN_DK] = (yh * inv).astype(p_ref.dtype)
            elif blk == 3:
                z = src[HALO + r0:HALO + r0 + HYB_EPI_ROWS, :]
                p_ref[rows, cols] = (z * _sigmoid(z)).astype(p_ref.dtype)
            elif blk == 4:
                xc = _causal_conv(lcw_ref.at[:, cols], src, HALO + r0, HYB_EPI_ROWS, LRU_CONV) + lcb_ref[:, cols]
                p_ref[rows, cols] = xc.astype(p_ref.dtype)
            else:
                gate = src[HALO + r0:HALO + r0 + HYB_EPI_ROWS, :]
                p_ref[rows, cols] = _gelu_tanh(gate).astype(p_ref.dtype)

    for blk in range(HYB_BLOCKS):
        @pl.when(j == blk)
        def _(blk=blk):
            for piece in range(npieces):
                matmul(blk, piece)
            for piece in range(npieces):
                epilogue(blk, piece)


def _hyb_in(x, g_all, w_t, gdn_cw, lru_cw, lru_cb, l, e, *, tm=1024, name):
    s, d = x.shape
    tm, tn = min(tm, s), HYB_TN
    nbytes = (2 * tm * d * 4 + (tm + HALO) * d * 2 + 4 * d * tn * 2 + (tm + HALO) * tn * 4
              + 2 * tm * tn * 2 + 8 * HYB_EPI_ROWS * HYB_PIECE * 4)
    return pl.pallas_call(
        _hyb_in_kernel, grid=(s // tm, HYB_BLOCKS), name=name,
        in_specs=[pl.BlockSpec((HALO, d), _halo_index(tm)),
                  pl.BlockSpec((tm, d), lambda i, j: (i, 0)),
                  pl.BlockSpec((None, 1, d), lambda i, j: (l, 0, 0)),
                  pl.BlockSpec((None, tn, d), lambda i, j: (e, jnp.minimum(j, 3), 0)),
                  pl.BlockSpec((pl.Element(1), pl.Element(tn), pl.Element(d)),
                               lambda i, j: (e, pl.multiple_of(HYB_LRU0 + tn * jnp.maximum(j - 4, 0),
                                                               V7X_BF16_ROWS), 0)),
                  pl.BlockSpec((None, V7X_LANES, d), lambda i, j: (e, HYB_BA0 // V7X_LANES, 0)),
                  pl.BlockSpec((None, GDN_CONV, tn), lambda i, j: (e, 0, jnp.minimum(j, 2))),
                  pl.BlockSpec((None, LRU_CONV, LRU_WIDTH), lambda i, j: (e, 0, 0)),
                  pl.BlockSpec((None, 1, LRU_WIDTH), lambda i, j: (e, 0, 0))],
        out_specs=[pl.BlockSpec((tm, tn), lambda i, j: (i, j)),
                   pl.BlockSpec((tm, V7X_LANES), lambda i, j: (i, 0))],
        out_shape=[jax.ShapeDtypeStruct((s, HYB_BLOCKS * tn), BF16),
                   jax.ShapeDtypeStruct((s, V7X_LANES), F32)],
        scratch_shapes=[pltpu.VMEM((tm + HALO, d), BF16)]
        + [pltpu.VMEM((tm + HALO, HYB_PIECE), F32) for _ in range(tn // HYB_PIECE)],
        compiler_params=_params(("parallel", "arbitrary"), nbytes),
    )(x, x, g_all, w_t, w_t, w_t, gdn_cw, lru_cw, lru_cb)


def _mem_kv_kernel(mem_ref, g_ref, w_ref, kg_ref, k_ref, v_ref):
    mn = _rms(mem_ref[...], g_ref[...]).astype(BF16)
    kv = _dot(mn, w_ref[...].astype(BF16))
    kg = kg_ref[...]
    for h in range(X_HEADS):
        kh = kv[:, h * X_HD:(h + 1) * X_HD]
        k_ref[:, h * X_HD:(h + 1) * X_HD] = _rms(kh, kg).astype(BF16)
    v_ref[...] = kv[:, X_INNER:].astype(BF16)


def _mem_kv(mem, norm_mem, xkv_w, xk_norm):
    m, d = mem.shape
    nl = norm_mem.shape[0]
    out = jax.ShapeDtypeStruct((nl, m, X_INNER), BF16)
    o_spec = pl.BlockSpec((None, m, X_INNER), lambda l: (l, 0, 0))
    return pl.pallas_call(
        _mem_kv_kernel, grid=(nl,), name="mem_kv",
        in_specs=[pl.BlockSpec((m, d), lambda l: (0, 0)),
                  pl.BlockSpec((None, 1, d), lambda l: (l, 0, 0)),
                  pl.BlockSpec((None, d, 2 * X_INNER), lambda l: (l, 0, 0)),
                  pl.BlockSpec((None, 1, X_HD), lambda l: (l, 0, 0))],
        out_specs=[o_spec, o_spec], out_shape=[out, out],
        compiler_params=_params(("parallel",), 2 * d * 2 * X_INNER * 2 + m * d * 4),
    )(mem, norm_mem, xkv_w, xk_norm)


def _mix_cross_kernel(*refs):
    *a_refs, wm_ref, x_ref, g_ref, wq_ref, qg_ref, k_ref, v_ref, wo_ref, o_ref = refs
    x = x_ref[...]
    k0 = 0
    for a_ref in a_refs:
        ka = a_ref.shape[1]
        x = x + _dot(a_ref[...], wm_ref[k0:k0 + ka, :])
        k0 += ka
    h = _rms(x, g_ref[...]).astype(BF16)
    q = _dot(h, wq_ref[...].astype(BF16))
    qg = qg_ref[...]
    outs = []
    for hh in range(X_HEADS):
        sl = slice(hh * X_HD, (hh + 1) * X_HD)
        qn = (_rms(q[:, sl], qg) * (X_HD ** -0.5 * LOG2E)).astype(BF16)
        sc = _dot_nt(qn, k_ref[:, sl])
        p = jnp.exp2(sc - jnp.max(sc, axis=-1, keepdims=True))
        inv = 1.0 / jnp.sum(p, axis=-1, keepdims=True)
        outs.append((_dot(p.astype(BF16), v_ref[:, sl]) * inv).astype(BF16))
    o = jnp.concatenate(outs, axis=-1)
    o_ref[...] = x + _dot(o, wo_ref[...].astype(BF16))


def _mix_cross_block(a_list, wm_all, wl, x, g_all, wq_all, qg_all, kn_all, v_all, wo_all, l, *, tm=512, name):
    s, d = x.shape
    tm = min(tm, s)
    m = kn_all.shape[1]
    kmix = wm_all.shape[1]
    nbytes = (4 * tm * d * 4 + 2 * tm * kmix * 2 + 2 * kmix * d * 2 + 4 * d * X_INNER * 4
              + 4 * m * X_INNER * 2 + 4 * tm * m * 4)
    layer = lambda shape: pl.BlockSpec((None,) + shape, lambda i: (l, 0, 0))
    return pl.pallas_call(
        _mix_cross_kernel, grid=(s // tm,), name=name,
        in_specs=[pl.BlockSpec((tm, a.shape[1]), lambda i: (i, 0)) for a in a_list]
        + [pl.BlockSpec((None, kmix, d), lambda i: (wl, 0, 0)),
           pl.BlockSpec((tm, d), lambda i: (i, 0)), layer((1, d)), layer((d, X_INNER)),
           layer((1, X_HD)), layer((m, X_INNER)), layer((m, X_INNER)), layer((X_INNER, d))],
        out_specs=pl.BlockSpec((tm, d), lambda i: (i, 0)),
        out_shape=jax.ShapeDtypeStruct((s, d), F32),
        compiler_params=_params(("parallel",), nbytes),
    )(*a_list, wm_all, x, g_all, wq_all, qg_all, kn_all, v_all, wo_all)


def _ffn_kernel(xh_ref, x_ref, g_ref, wg_ref, wu_ref, cw_ref, cb_ref, wo_ref, o_ref, h_ref, gt_ref):
    tm = x_ref.shape[0]

    @pl.when(pl.program_id(1) == 0)
    def _():
        g = g_ref[...]
        keep = (pl.program_id(0) > 0).astype(F32)
        h_ref[0:HALO, :] = _rms(xh_ref[...] * keep, g).astype(BF16)
        x = x_ref[...]
        h_ref[HALO:, :] = _rms(x, g).astype(BF16)
        o_ref[...] = x

    gt_ref[...] = _dot(h_ref[...], wg_ref[...])
    up = _dot(h_ref[HALO:, :], wu_ref[...])
    y = _causal_conv(cw_ref, gt_ref, HALO, tm, FFN_CONV) + cb_ref[...]
    act = (y * _sigmoid(y) * up).astype(BF16)
    o_ref[...] += _dot(act, wo_ref[...])


def _ffn_block(x, g_all, w_in_all, cw_all, cb_all, w_out_all, l, *, tm=1024, tf=512, name):
    s, d = x.shape
    tm = min(tm, s)
    nf = D_FF // tf
    nbytes = 4 * tm * d * 4 + (tm + HALO) * d * 2 + 6 * d * tf * 2 + 3 * (tm + HALO) * tf * 4
    return pl.pallas_call(
        _ffn_kernel, grid=(s // tm, nf), name=name,
        in_specs=[pl.BlockSpec((HALO, d), _halo_index(tm)),
                  pl.BlockSpec((tm, d), lambda i, j: (i, 0)),
                  pl.BlockSpec((None, 1, d), lambda i, j: (l, 0, 0)),
                  pl.BlockSpec((None, d, tf), lambda i, j: (l, 0, j)),
                  pl.BlockSpec((None, d, tf), lambda i, j: (l, 0, j + nf)),
                  pl.BlockSpec((None, FFN_CONV, tf), lambda i, j: (l, 0, j)),
                  pl.BlockSpec((None, 1, tf), lambda i, j: (l, 0, j)),
                  pl.BlockSpec((None, tf, d), lambda i, j: (l, j, 0))],
        out_specs=pl.BlockSpec((tm, d), lambda i, j: (i, 0)),
        out_shape=jax.ShapeDtypeStruct((s, d), F32),
        scratch_shapes=[pltpu.VMEM((tm + HALO, d), BF16), pltpu.VMEM((tm + HALO, tf), F32)],
        compiler_params=_params(("parallel", "arbitrary"), nbytes),
    )(x, x, g_all, w_in_all, w_in_all, cw_all, cb_all, w_out_all)


GDN_STEP_CHUNKS = 4
NEUMANN_DOUBLINGS = int(math.log2(GDN_CHUNK)) - 1


def _gdn_kernel(q_ref, k_ref, v_ref, z_ref, ba_ref, alog_ref, dt_ref, ng_ref, o_ref, st_ref):
    c = GDN_CHUNK
    rows = q_ref.shape[0]
    nchunks = rows // c

    @pl.when(pl.program_id(0) == 0)
    def _():
        st_ref[...] = jnp.zeros_like(st_ref)

    ba = ba_ref[...]
    lane = lax.broadcasted_iota(jnp.int32, ba.shape, 1)
    row_in_chunk = lax.broadcasted_iota(jnp.int32, ba.shape, 0) % c
    bg = jnp.where(lane < GDN_HEADS, _sigmoid(ba), -jnp.exp(alog_ref[...]) * _softplus(ba + dt_ref[...]))
    gcs = bg
    shift = 1
    while shift < c:
        gcs = gcs + jnp.where(row_in_chunk >= shift, pltpu.roll(gcs, shift, 0), 0.0)
        shift *= 2
    egcs = jnp.exp(gcs)

    ri = lax.broadcasted_iota(jnp.int32, (c, c), 0)
    ci = lax.broadcasted_iota(jnp.int32, (c, c), 1)
    causal = ri >= ci
    strict = ri > ci
    eye = (ri == ci).astype(F32)
    ng = ng_ref[...]

    heads = range(GDN_HEADS)
    chains = [(n, h) for n in range(nchunks) for h in heads]
    rsl = lambda n: slice(n * c, (n + 1) * c)
    hsl = lambda h: slice(h * GDN_DK, (h + 1) * GDN_DK)
    gct = [jnp.concatenate([gcs[rsl(n)], jnp.zeros((c, V7X_LANES), F32)], axis=0).T for n in range(nchunks)]

    kf, beta, gc_col, egc, g_last, decay, kb = {}, {}, {}, {}, {}, {}, {}
    for n, h in chains:
        kf[n, h] = k_ref[rsl(n), hsl(h)].astype(F32)
        beta[n, h] = bg[rsl(n), h:h + 1]
        gc_col[n, h] = gcs[rsl(n), GDN_HEADS + h:GDN_HEADS + h + 1]
        egc[n, h] = egcs[rsl(n), GDN_HEADS + h:GDN_HEADS + h + 1]
        gc_row = gct[n][GDN_HEADS + h:GDN_HEADS + h + 1, 0:c]
        g_last[n, h] = gc_row[:, c - 1:c]
        decay[n, h] = jnp.where(causal, jnp.exp(jnp.minimum(gc_col[n, h] - gc_row, 0.0)), 0.0)
        kb[n, h] = kf[n, h] * beta[n, h]
    qk = {ch: _dot_nt(jnp.concatenate([q_ref[rsl(ch[0]), hsl(ch[1])], kb[ch].astype(BF16)], axis=0),
                      k_ref[rsl(ch[0]), hsl(ch[1])]) for ch in chains}
    attn = {ch: (qk[ch][0:c] * decay[ch]).astype(BF16) for ch in chains}
    lmat = {ch: jnp.where(strict, qk[ch][c:2 * c] * decay[ch], 0.0) for ch in chains}
    pm = {ch: eye - lmat[ch] for ch in chains}
    mpow = {ch: lmat[ch].astype(BF16) for ch in chains}
    for _ in range(NEUMANN_DOUBLINGS):
        mpow = {ch: _dot(mpow[ch], mpow[ch]).astype(BF16) for ch in chains}
        pm = {ch: pm[ch] + _dot(pm[ch].astype(BF16), mpow[ch]) for ch in chains}
    uw = {ch: _dot(pm[ch].astype(BF16),
                   jnp.concatenate([v_ref[rsl(ch[0]), hsl(ch[1])].astype(F32) * beta[ch], kb[ch] * egc[ch]],
                                   axis=1).astype(BF16)) for ch in chains}
    wq = {ch: jnp.concatenate([uw[ch][:, GDN_DV:],
                               q_ref[rsl(ch[0]), hsl(ch[1])].astype(F32) * egc[ch]], axis=0).astype(BF16)
          for ch in chains}
    k_dec = {ch: (kf[ch] * jnp.exp(g_last[ch] - gc_col[ch])).astype(BF16) for ch in chains}

    state = [st_ref[h] for h in heads]
    for n in range(nchunks):
        ws = [_dot(wq[n, h], state[h].astype(BF16)) for h in heads]
        v_new = [(uw[n, h][:, 0:GDN_DV] - ws[h][0:c]).astype(BF16) for h in heads]
        state = [state[h] * jnp.exp(g_last[n, h]) + _dot_tn(k_dec[n, h], v_new[h]) for h in heads]
        o = [ws[h][c:2 * c] + _dot(attn[n, h], v_new[h]) for h in heads]
        for h in heads:
            o_ref[rsl(n), hsl(h)] = (_rms(o[h], ng) * z_ref[rsl(n), hsl(h)].astype(F32)).astype(o_ref.dtype)
    for h in heads:
        st_ref[h] = state[h]


def _gdn_mixer(p, ba, alog_all, dt_all, ng_all, e, *, name):
    s = p.shape[0]
    rows = GDN_STEP_CHUNKS * GDN_CHUNK
    col = lambda b: pl.BlockSpec((rows, GDN_V), lambda i: (i, b))
    layer = lambda shape: pl.BlockSpec((None,) + shape, lambda i: (e, 0, 0))
    return pl.pallas_call(
        _gdn_kernel, grid=(s // rows,), name=name,
        in_specs=[col(0), col(1), col(2), col(3),
                  pl.BlockSpec((rows, V7X_LANES), lambda i: (i, 0)),
                  layer((1, V7X_LANES)), layer((1, V7X_LANES)), layer((1, GDN_DV))],
        out_specs=pl.BlockSpec((rows, GDN_V), lambda i: (i, 0)),
        out_shape=jax.ShapeDtypeStruct((s, GDN_V), BF16),
        scratch_shapes=[pltpu.VMEM((GDN_HEADS, GDN_DK, GDN_DV), F32)],
        compiler_params=_params(("arbitrary",), 16 * rows * GDN_V * 4),
    )(p, p, p, p, ba, alog_all, dt_all, ng_all)


def _lru_kernel(xc_ref, gate_ref, wr_ref, br_ref, wi_ref, bi_ref, lam_ref, o_ref, a_ref, b_ref, carry_ref):
    tt = xc_ref.shape[0]

    @pl.when(pl.program_id(0) == 0)
    def _():
        carry_ref[...] = jnp.zeros_like(carry_ref)

    for blk in range(LRU_BLOCKS):
        sl = slice(blk * LRU_BLOCK, (blk + 1) * LRU_BLOCK)
        xb = xc_ref[:, sl]
        r = _sigmoid_tanh(_dot(xb, wr_ref[blk]) + br_ref[:, sl])
        i = _sigmoid_tanh(_dot(xb, wi_ref[blk]) + bi_ref[:, sl])
        log_a = (-LRU_C) * r * _softplus(-lam_ref[:, sl])
        a = jnp.exp(log_a)
        one_minus_a2 = -jnp.tanh(log_a) * (1.0 + a * a)
        a_ref[:, sl] = a
        b_ref[:, sl] = jnp.sqrt(one_minus_a2) * (i * xb.astype(F32))

    row = lax.broadcasted_iota(jnp.int32, (V7X_SUBLANES, LRU_WIDTH), 0)

    def group(gi, hprev):
        r0 = pl.multiple_of(gi * V7X_SUBLANES, V7X_SUBLANES)
        a = a_ref[pl.ds(r0, V7X_SUBLANES), :]
        b = b_ref[pl.ds(r0, V7X_SUBLANES), :]
        shift = 1
        while shift < V7X_SUBLANES:
            keep = row >= shift
            b = b + a * jnp.where(keep, pltpu.roll(b, shift, 0), 0.0)
            a = a * jnp.where(keep, pltpu.roll(a, shift, 0), 1.0)
            shift *= 2
        hcur = b + a * hprev
        b_ref[pl.ds(r0, V7X_SUBLANES), :] = hcur
        return jnp.broadcast_to(hcur[V7X_SUBLANES - 1:V7X_SUBLANES, :], hcur.shape)

    carry_ref[...] = lax.fori_loop(0, tt // V7X_SUBLANES, group, carry_ref[...])
    o_ref[...] = (b_ref[...] * gate_ref[...].astype(F32)).astype(o_ref.dtype)


def _lru_mixer(p, wr_all, br_all, wi_all, bi_all, lam_all, e, *, tt=256, name):
    s = p.shape[0]
    tt = min(tt, s)
    vec = pl.BlockSpec((None, 1, LRU_WIDTH), lambda i: (e, 0, 0))
    mat = pl.BlockSpec((None, LRU_BLOCKS, LRU_BLOCK, LRU_BLOCK), lambda i: (e, 0, 0, 0))
    return pl.pallas_call(
        _lru_kernel, grid=(s // tt,), name=name,
        in_specs=[pl.BlockSpec((tt, LRU_WIDTH), lambda i: (i, 4)),
                  pl.BlockSpec((tt, LRU_WIDTH), lambda i: (i, 5)),
                  mat, vec, mat, vec, vec],
        out_specs=pl.BlockSpec((tt, LRU_WIDTH), lambda i: (i, 0)),
        out_shape=jax.ShapeDtypeStruct((s, LRU_WIDTH), BF16),
        scratch_shapes=[pltpu.VMEM((tt, LRU_WIDTH), F32),
                        pltpu.VMEM((tt, LRU_WIDTH), F32),
                        pltpu.VMEM((V7X_SUBLANES, LRU_WIDTH), F32)],
        compiler_params=_params(("arbitrary",), 10 * tt * LRU_WIDTH * 4),
    )(p, p, wr_all, br_all, wi_all, bi_all, lam_all)


def _rope_kernel(pos_ref, inv_ref, cos_ref, sin_ref):
    ang = pos_ref[...].astype(F32) * inv_ref[...]
    cos_ref[...] = jnp.cos(ang)
    sin_ref[...] = jnp.sin(ang)


def _rope_tables(pos_col, inv_lanes, *, tm=1024):
    s = pos_col.shape[0]
    tm = min(tm, s)
    out = jax.ShapeDtypeStruct((s, V7X_LANES), F32)
    o_spec = pl.BlockSpec((tm, V7X_LANES), lambda i: (i, 0))
    return pl.pallas_call(
        _rope_kernel, grid=(s // tm,), name="rope_tables",
        in_specs=[pl.BlockSpec((tm, 1), lambda i: (i, 0)), pl.BlockSpec((1, V7X_LANES), lambda i: (0, 0))],
        out_specs=[o_spec, o_spec], out_shape=[out, out],
        compiler_params=_params(("parallel",), 8 * tm * V7X_LANES * 4),
    )(pos_col, inv_lanes)


SWA_PAIR = 2 * V7X_LANES
SWA_Q_PAIRS = SWA_Q // SWA_PAIR
SWA_KV_PAIRS = SWA_KV // SWA_PAIR
SWA_GROUPS_PER_PAIR = V7X_LANES // SWA_HALF


def _swa_kernel(sink_ref, qkv_ref, cos_ref, sin_ref, qg_ref, kg_ref, seg_ref, o_ref, k_ref, v_ref):
    blk = SWA_BLOCK
    first = pl.program_id(0) == 0

    @pl.when(first)
    def _():
        k_ref[0:blk, :] = jnp.zeros((blk, SWA_KV), BF16)
        v_ref[0:blk, :] = jnp.zeros((blk, SWA_KV), BF16)

    @pl.when(jnp.logical_not(first))
    def _():
        k_ref[0:blk, :] = k_ref[blk:2 * blk, :]
        v_ref[0:blk, :] = v_ref[blk:2 * blk, :]

    cos = cos_ref[...]
    sin = sin_ref[...]
    seg = seg_ref[...]

    def norm_rope(col0, gain_ref):
        a = qkv_ref[:, col0:col0 + V7X_LANES]
        b = qkv_ref[:, col0 + V7X_LANES:col0 + SWA_PAIR]
        sq = a * a + b * b
        hi = sq.astype(BF16)
        lo = (sq - hi.astype(F32)).astype(BF16)
        ss = _dot(hi, seg) + _dot(lo, seg)
        scale = lax.rsqrt(ss * (1.0 / SWA_HD) + EPS)
        a = a * scale * gain_ref[:, 0:V7X_LANES]
        b = b * scale * gain_ref[:, V7X_LANES:SWA_PAIR]
        return a * cos - b * sin, b * cos + a * sin

    for u in range(SWA_KV_PAIRS):
        ka, kb = norm_rope(SWA_Q + u * SWA_PAIR, kg_ref)
        k_ref[blk:2 * blk, u * SWA_PAIR:u * SWA_PAIR + V7X_LANES] = ka.astype(BF16)
        k_ref[blk:2 * blk, u * SWA_PAIR + V7X_LANES:(u + 1) * SWA_PAIR] = kb.astype(BF16)
    v_ref[blk:2 * blk, :] = qkv_ref[:, SWA_Q + SWA_KV:SWA_Q + 2 * SWA_KV].astype(BF16)

    qi = lax.broadcasted_iota(jnp.int32, (blk, 2 * blk), 0)
    kj = lax.broadcasted_iota(jnp.int32, (blk, 2 * blk), 1)
    lo_key = jnp.where(first, blk, 0)
    valid = (kj > qi) & (kj <= qi + SWA_WINDOW) & (kj >= lo_key)
    lane_group = lax.broadcasted_iota(jnp.int32, (blk, SWA_PAIR), 1) % V7X_LANES // SWA_HALF

    pairs = range(SWA_Q_PAIRS)
    groups = range(SWA_GROUPS_PER_PAIR)
    kv_pair = lambda t: t // SWA_GROUPS_PER_PAIR
    kv_cols = lambda t: slice(kv_pair(t) * SWA_PAIR, (kv_pair(t) + 1) * SWA_PAIR)
    qs = []
    for t in pairs:
        qa, qb = norm_rope(t * SWA_PAIR, qg_ref)
        qp = jnp.concatenate([qa, qb], axis=1) * (SWA_HD ** -0.5 * LOG2E)
        qs.append(jnp.concatenate([jnp.where(lane_group == gl, qp, 0.0) for gl in groups],
                                  axis=0).astype(BF16))
    sc_all = [_dot_nt(qs[t], k_ref[:, kv_cols(t)]) for t in pairs]
    ps, invs = [], []
    for t in pairs:
        pt, it = [], []
        for gl in groups:
            head = SWA_GROUP * (SWA_GROUPS_PER_PAIR * kv_pair(t) + gl) + t % SWA_GROUPS_PER_PAIR
            sink = sink_ref[head] * LOG2E
            sc = jnp.where(valid, sc_all[t][gl * blk:(gl + 1) * blk], -jnp.inf)
            m = jnp.maximum(jnp.max(sc, axis=-1, keepdims=True), sink)
            p = jnp.exp2(sc - m)
            it.append(1.0 / (jnp.sum(p, axis=-1, keepdims=True) + jnp.exp2(sink - m)))
            pt.append(p.astype(BF16))
        ps.append(jnp.concatenate(pt, axis=0))
        invs.append(it)
    pv = [_dot(ps[t], v_ref[:, kv_cols(t)]) for t in pairs]
    for t in pairs:
        out = jnp.zeros((blk, SWA_PAIR), F32)
        for gl in groups:
            out = jnp.where(lane_group == gl, pv[t][gl * blk:(gl + 1) * blk] * invs[t][gl], out)
        o_ref[:, t * SWA_PAIR:(t + 1) * SWA_PAIR] = out.astype(o_ref.dtype)


def _swa_mixer(qkv, sinks_all, cos, sin, qg_all, kg_all, seg, o, *, name):
    s, n = qkv.shape
    blk = SWA_BLOCK
    layer = lambda shape: pl.BlockSpec((None,) + shape, lambda i: (o, 0, 0))
    return pl.pallas_call(
        _swa_kernel, grid=(s // blk,), name=name,
        in_specs=[pl.BlockSpec(memory_space=pltpu.SMEM),
                  pl.BlockSpec((blk, n), lambda i: (i, 0)),
                  pl.BlockSpec((blk, V7X_LANES), lambda i: (i, 0)),
                  pl.BlockSpec((blk, V7X_LANES), lambda i: (i, 0)),
                  layer((1, SWA_PAIR)), layer((1, SWA_PAIR)),
                  pl.BlockSpec((V7X_LANES, V7X_LANES), lambda i: (0, 0))],
        out_specs=pl.BlockSpec((blk, SWA_Q), lambda i: (i, 0)),
        out_shape=jax.ShapeDtypeStruct((s, SWA_Q), BF16),
        scratch_shapes=[pltpu.VMEM((2 * blk, SWA_KV), BF16), pltpu.VMEM((2 * blk, SWA_KV), BF16)],
        compiler_params=_params(("arbitrary",), 8 * blk * n * 4),
    )(sinks_all[o], qkv, cos, sin, qg_all, kg_all, seg)


def _swa_q_src(t, half, gl):
    head = SWA_GROUPS_PER_PAIR * SWA_GROUP * (t // SWA_GROUPS_PER_PAIR) + SWA_GROUP * gl + t % SWA_GROUPS_PER_PAIR
    return head * SWA_HD + half * SWA_HALF


def _swa_kv_src(u, half, gl):
    return (SWA_GROUPS_PER_PAIR * u + gl) * SWA_HD + half * SWA_HALF


def _swa_in_prep_kernel(w_ref, o_ref):
    groups = range(SWA_GROUPS_PER_PAIR)

    def tile(srcs):
        return jnp.concatenate([w_ref[:, s:s + SWA_HALF] for s in srcs], axis=1).astype(BF16)

    for t in range(SWA_Q_PAIRS):
        for half in range(2):
            c0 = t * SWA_PAIR + half * V7X_LANES
            o_ref[:, c0:c0 + V7X_LANES] = tile([_swa_q_src(t, half, gl) for gl in groups])
    for base in (SWA_Q, SWA_Q + SWA_KV):
        for u in range(SWA_KV_PAIRS):
            for half in range(2):
                c0 = base + u * SWA_PAIR + half * V7X_LANES
                o_ref[:, c0:c0 + V7X_LANES] = tile([base + _swa_kv_src(u, half, gl) for gl in groups])


def _swa_in_prep(w, *, tr=256):
    n_layers, d, n = w.shape
    spec = pl.BlockSpec((None, tr, n), lambda e, i: (e, i, 0))
    return pl.pallas_call(
        _swa_in_prep_kernel, grid=(n_layers, d // tr), name="swa_in_prep",
        in_specs=[spec], out_specs=spec, out_shape=jax.ShapeDtypeStruct(w.shape, BF16),
        compiler_params=_params(("parallel", "parallel"), 2 * tr * n * 6),
    )(w)


def _swa_out_prep_kernel(w_ref, o_ref):
    for t in range(SWA_Q_PAIRS):
        for half in range(2):
            for gl in range(SWA_GROUPS_PER_PAIR):
                new = t * SWA_PAIR + half * V7X_LANES + gl * SWA_HALF
                src = _swa_q_src(t, half, gl)
                o_ref[new:new + SWA_HALF, :] = w_ref[src:src + SWA_HALF, :].astype(BF16)


def _swa_out_prep(w, *, tc=1024):
    n_layers, k, d = w.shape
    spec = pl.BlockSpec((None, k, tc), lambda e, j: (e, 0, j))
    return pl.pallas_call(
        _swa_out_prep_kernel, grid=(n_layers, d // tc), name="swa_out_prep",
        in_specs=[spec], out_specs=spec, out_shape=jax.ShapeDtypeStruct(w.shape, BF16),
        compiler_params=_params(("parallel", "parallel"), 2 * k * tc * 6),
    )(w)


def _gain_lanes(g):
    n = g.shape[0]
    g = jnp.broadcast_to(g.reshape(n, 2, 1, SWA_HALF), (n, 2, SWA_GROUPS_PER_PAIR, SWA_HALF))
    return g.reshape(n, 1, SWA_PAIR).astype(F32)


def _lane_rows(vals, offset):
    n, m = vals.shape
    return jnp.pad(vals.astype(F32), ((0, 0), (offset, V7X_LANES - offset - m))).reshape(n, 1, V7X_LANES)


def kernel(x, mem, positions, norm_mix, norm_cross, norm_mem, norm_ffn, xq_w, xkv_w, xo_w, xq_norm, xk_norm, ffn_in_w, ffn_conv_w, ffn_conv_b, ffn_out_w, hyb_in_w, hyb_out_w, gdn_conv_w, gdn_a_log, gdn_dt_bias, gdn_norm, lru_conv_w, lru_conv_b, lru_wr, lru_br, lru_wi, lru_bi, lru_lambda, swa_in_w, swa_out_w, swa_q_norm, swa_k_norm, swa_sinks):
    bsz, s, d = x.shape
    assert bsz == 1 and d == D_MODEL and s % 1024 == 0
    xs = x[0]
    rows3 = lambda v: v.reshape(v.shape[0], 1, v.shape[1]).astype(F32)

    norm_mix3, norm_cross3, norm_ffn3 = rows3(norm_mix), rows3(norm_cross), rows3(norm_ffn)
    hyb_t = hyb_in_w.transpose(0, 2, 1).astype(BF16)
    hyb_out_b = hyb_out_w.astype(BF16)
    alog3, dt3 = _lane_rows(gdn_a_log, GDN_HEADS), _lane_rows(gdn_dt_bias, GDN_HEADS)
    gdn_norm3 = rows3(gdn_norm)
    lru_cb3, lru_br3, lru_bi3, lru_lam3 = rows3(lru_conv_b), rows3(lru_br), rows3(lru_bi), rows3(lru_lambda)
    lru_wr_b, lru_wi_b = lru_wr.astype(BF16), lru_wi.astype(BF16)
    swa_in_b = _swa_in_prep(swa_in_w)
    swa_out_b = _swa_out_prep(swa_out_w)
    swa_qg, swa_kg = _gain_lanes(swa_q_norm), _gain_lanes(swa_k_norm)
    lanes = jnp.arange(V7X_LANES)
    seg = (lanes[:, None] // SWA_HALF == lanes[None, :] // SWA_HALF).astype(BF16)
    xq_norm3 = rows3(xq_norm)
    ffn_in_b, ffn_out_b = ffn_in_w.astype(BF16), ffn_out_w.astype(BF16)
    ffn_cb3 = rows3(ffn_conv_b)

    inv = 1.0 / (ROPE_THETA ** (jnp.arange(0, SWA_HD, 2, dtype=F32) / SWA_HD))
    inv_lanes = jnp.tile(inv, SWA_GROUPS_PER_PAIR).reshape(1, V7X_LANES)
    cos, sin = _rope_tables(positions.reshape(s, 1), inv_lanes)
    kn_all, v_all = _mem_kv(mem[0], rows3(norm_mem), xkv_w, rows3(xk_norm))

    for l in range(DEPTH):
        if l % 2 == 0:
            e = l // 2
            p, ba = _hyb_in(xs, norm_mix3, hyb_t, gdn_conv_w, lru_conv_w, lru_cb3, l, e,
                            name=f"hyb_in_{l}")
            oa = _gdn_mixer(p, ba, alog3, dt3, gdn_norm3, e, name=f"gdn_{l}")
            ob = _lru_mixer(p, lru_wr_b, lru_br3, lru_wi_b, lru_bi3, lru_lam3, e, name=f"lru_{l}")
            mixed, w_mix, wl = [oa, ob], hyb_out_b, e
        else:
            o = l // 2
            qkv = _norm_matmul(xs, norm_mix3, swa_in_b, l, o, name=f"swa_in_{l}")
            att = _swa_mixer(qkv, swa_sinks.astype(F32), cos, sin, swa_qg, swa_kg, seg, o, name=f"swa_{l}")
            mixed, w_mix, wl = [att], swa_out_b, o
        xs = _mix_cross_block(mixed, w_mix, wl, xs, norm_cross3, xq_w, xq_norm3, kn_all, v_all, xo_w, l,
                              name=f"mix_cross_{l}")
        xs = _ffn_block(xs, norm_ffn3, ffn_in_b, ffn_conv_w, ffn_cb3, ffn_out_b, l, name=f"ffn_{l}")
    return xs[None]
```

```python
import math

import jax
import jax.numpy as jnp
from jax import lax
from jax.experimental import pallas as pl
from jax.experimental.pallas import tpu as pltpu

F32 = jnp.float32
BF16 = jnp.bfloat16

D_MODEL = 2048
DEPTH = 4
MEM_LEN = 256
EPS = 1e-6
GDN_HEADS = 8
GDN_DK = 128
GDN_DV = 128
GDN_CONV = 4
GDN_CHUNK = 64
GDN_QK = GDN_HEADS * GDN_DK
GDN_V = GDN_HEADS * GDN_DV
GDN_QKV = 2 * GDN_QK + GDN_V
LRU_WIDTH = D_MODEL // 2
LRU_BLOCKS = 8
LRU_BLOCK = LRU_WIDTH // LRU_BLOCKS
LRU_CONV = 4
LRU_C = 8.0
SWA_HEADS = 32
SWA_KV_HEADS = 8
SWA_GROUP = SWA_HEADS // SWA_KV_HEADS
SWA_HD = 64
SWA_HALF = SWA_HD // 2
SWA_WINDOW = 128
SWA_BLOCK = 128
SWA_Q = SWA_HEADS * SWA_HD
SWA_KV = SWA_KV_HEADS * SWA_HD
ROPE_THETA = 10000.0
X_HEADS = 4
X_HD = 128
X_INNER = X_HEADS * X_HD
D_FF = 5632
FFN_CONV = 3
LOG2E = math.log2(math.e)

V7X_LANES = 128
V7X_SUBLANES = 8
V7X_BF16_ROWS = 16
V7X_VMEM_BYTES = 64 * 1024 * 1024

HALO = V7X_BF16_ROWS


def _vmem_limit(nbytes):
    return int(min(max(2 * nbytes, 16 * 1024 * 1024), V7X_VMEM_BYTES - 8 * 1024 * 1024))


def _params(semantics, nbytes):
    return pltpu.CompilerParams(dimension_semantics=semantics, vmem_limit_bytes=_vmem_limit(nbytes))


def _rms(x, g):
    return x * lax.rsqrt(jnp.mean(x * x, axis=-1, keepdims=True) + EPS) * g


def _sigmoid(x):
    return 1.0 / (1.0 + jnp.exp(-x))


def _sigmoid_tanh(x):
    return 0.5 * jnp.tanh(0.5 * x) + 0.5


def _softplus(x):
    return jnp.maximum(x, 0.0) + jnp.log1p(jnp.exp(-jnp.abs(x)))


def _gelu_tanh(x):
    return 0.5 * x * (1.0 + jnp.tanh(math.sqrt(2.0 / math.pi) * (x + 0.044715 * (x * x * x))))


def _dot(a, b):
    return jnp.dot(a, b, preferred_element_type=F32)


def _dot_nt(a, b):
    return lax.dot_general(a, b, (((1,), (1,)), ((), ())), preferred_element_type=F32)


def _dot_tn(a, b):
    return lax.dot_general(a, b, (((0,), (0,)), ((), ())), preferred_element_type=F32)


def _halo_index(tm):
    per_tile = tm // HALO
    return lambda i, j: (jnp.maximum(i * per_tile - 1, 0), 0)


def _causal_conv(cw_ref, src_ref, row0, rows, width):
    win = src_ref[row0 - V7X_SUBLANES:row0 + rows, :]
    y = cw_ref[width - 1:width, :] * win[V7X_SUBLANES:, :]
    for j in range(width - 1):
        back = width - 1 - j
        y = y + cw_ref[j:j + 1, :] * pltpu.roll(win, back, 0)[V7X_SUBLANES:, :]
    return y


def _norm_matmul_kernel(x_ref, g_ref, w_ref, o_ref, h_ref):
    @pl.when(pl.program_id(1) == 0)
    def _():
        h_ref[...] = _rms(x_ref[...], g_ref[...]).astype(BF16)

    o_ref[...] = _dot(h_ref[...], w_ref[...]).astype(o_ref.dtype)


def _norm_matmul(x, g_all, w_all, l, wl, *, tm=1024, tn=1536, name):
    s, d = x.shape
    n = w_all.shape[-1]
    tm, tn = min(tm, s), min(tn, n)
    nbytes = 2 * tm * d * 4 + tm * d * 2 + 2 * d * tn * 2 + 2 * tm * tn * 4
    return pl.pallas_call(
        _norm_matmul_kernel, grid=(s // tm, n // tn), name=name,
        in_specs=[pl.BlockSpec((tm, d), lambda i, j: (i, 0)),
                  pl.BlockSpec((None, 1, d), lambda i, j: (l, 0, 0)),
                  pl.BlockSpec((None, d, tn), lambda i, j: (wl, 0, j))],
        out_specs=pl.BlockSpec((tm, tn), lambda i, j: (i, j)),
        out_shape=jax.ShapeDtypeStruct((s, n), F32),
        scratch_shapes=[pltpu.VMEM((tm, d), BF16)],
        compiler_params=_params(("parallel", "arbitrary"), nbytes),
    )(x, g_all, w_all)


HYB_TN = 1024
HYB_BLOCKS = 6
HYB_BA0 = GDN_QKV + GDN_V
HYB_LRU0 = HYB_BA0 + 2 * GDN_HEADS
HYB_EPI_ROWS = 256


HYB_PIECE = 4 * V7X_LANES


def _hyb_in_kernel(xh_ref, x_ref, g_ref, wg_ref, wl_ref, wba_ref, gcw_ref, lcw_ref, lcb_ref,
                   p_ref, ba_ref, h_ref, *pe_refs):
    tm = x_ref.shape[0]
    j = pl.program_id(1)
    tiles = range(0, tm, HYB_EPI_ROWS)
    npieces = len(pe_refs)

    @pl.when(j == 0)
    def _():
        g = g_ref[...]
        keep = (pl.program_id(0) > 0).astype(F32)
        h_ref[0:HALO, :] = _rms(xh_ref[...] * keep, g).astype(BF16)
        hb = _rms(x_ref[...], g).astype(BF16)
        h_ref[HALO:, :] = hb
        ba_ref[...] = _dot_nt(hb, wba_ref[...])

    def matmul(blk, piece):
        w_ref = wg_ref if blk < 4 else wl_ref.at[0]
        pe_refs[piece][...] = _dot_nt(h_ref[...], w_ref[piece * HYB_PIECE:(piece + 1) * HYB_PIECE, :])

    def epilogue(blk, piece):
        src = pe_refs[piece]
        cols = slice(piece * HYB_PIECE, (piece + 1) * HYB_PIECE)
        for r0 in tiles:
            rows = slice(r0, r0 + HYB_EPI_ROWS)
            if blk <= 2:
                y = _causal_conv(gcw_ref.at[:, cols], src, HALO + r0, HYB_EPI_ROWS, GDN_CONV)
                y = y * _sigmoid(y)
                if blk == 2:
                    p_ref[rows, cols] = y.astype(p_ref.dtype)
                    continue
                qscale = GDN_DK ** -0.5 if blk == 0 else 1.0
                for h in range(HYB_PIECE // GDN_DK):
                    yh = y[:, h * GDN_DK:(h + 1) * GDN_DK]
                    inv = lax.rsqrt(jnp.sum(yh * yh, axis=-1, keepdims=True) + EPS) * qscale
                    c0 = piece * HYB_PIECE + h * GDN_DK
                    p_ref[rows, c0:c0 + GDN_DK] = (yh * inv).astype(p_ref.dtype)
            elif blk == 3:
                z = src[HALO + r0:HALO + r0 + HYB_EPI_ROWS, :]
                p_ref[rows, cols] = (z * _sigmoid(z)).astype(p_ref.dtype)
            elif blk == 4:
                xc = _causal_conv(lcw_ref.at[:, cols], src, HALO + r0, HYB_EPI_ROWS, LRU_CONV) + lcb_ref[:, cols]
                p_ref[rows, cols] = xc.astype(p_ref.dtype)
            else:
                gate = src[HALO + r0:HALO + r0 + HYB_EPI_ROWS, :]
                p_ref[rows, cols] = _gelu_tanh(gate).astype(p_ref.dtype)

    for blk in range(HYB_BLOCKS):
        @pl.when(j == blk)
        def _(blk=blk):
            for piece in range(npieces):
                matmul(blk, piece)
            for piece in range(npieces):
                epilogue(blk, piece)


def _hyb_in(x, g_all, w_t, gdn_cw, lru_cw, lru_cb, l, e, *, tm=1024, name):
    s, d = x.shape
    tm, tn = min(tm, s), HYB_TN
    nbytes = (2 * tm * d * 4 + (tm + HALO) * d * 2 + 4 * d * tn * 2 + (tm + HALO) * tn * 4
              + 2 * tm * tn * 2 + 8 * HYB_EPI_ROWS * HYB_PIECE * 4)
    return pl.pallas_call(
        _hyb_in_kernel, grid=(s // tm, HYB_BLOCKS), name=name,
        in_specs=[pl.BlockSpec((HALO, d), _halo_index(tm)),
                  pl.BlockSpec((tm, d), lambda i, j: (i, 0)),
                  pl.BlockSpec((None, 1, d), lambda i, j: (l, 0, 0)),
                  pl.BlockSpec((None, tn, d), lambda i, j: (e, jnp.minimum(j, 3), 0)),
                  pl.BlockSpec((pl.Element(1), pl.Element(tn), pl.Element(d)),
                               lambda i, j: (e, pl.multiple_of(HYB_LRU0 + tn * jnp.maximum(j - 4, 0),
                                                               V7X_BF16_ROWS), 0)),
                  pl.BlockSpec((None, V7X_LANES, d), lambda i, j: (e, HYB_BA0 // V7X_LANES, 0)),
                  pl.BlockSpec((None, GDN_CONV, tn), lambda i, j: (e, 0, jnp.minimum(j, 2))),
                  pl.BlockSpec((None, LRU_CONV, LRU_WIDTH), lambda i, j: (e, 0, 0)),
                  pl.BlockSpec((None, 1, LRU_WIDTH), lambda i, j: (e, 0, 0))],
        out_specs=[pl.BlockSpec((tm, tn), lambda i, j: (i, j)),
                   pl.BlockSpec((tm, V7X_LANES), lambda i, j: (i, 0))],
        out_shape=[jax.ShapeDtypeStruct((s, HYB_BLOCKS * tn), BF16),
                   jax.ShapeDtypeStruct((s, V7X_LANES), F32)],
        scratch_shapes=[pltpu.VMEM((tm + HALO, d), BF16)]
        + [pltpu.VMEM((tm + HALO, HYB_PIECE), F32) for _ in range(tn // HYB_PIECE)],
        compiler_params=_params(("parallel", "arbitrary"), nbytes),
    )(x, x, g_all, w_t, w_t, w_t, gdn_cw, lru_cw, lru_cb)


def _mem_kv_kernel(mem_ref, g_ref, w_ref, kg_ref, k_ref, v_ref):
    mn = _rms(mem_ref[...], g_ref[...]).astype(BF16)
    kv = _dot(mn, w_ref[...].astype(BF16))
    kg = kg_ref[...]
    for h in range(X_HEADS):
        kh = kv[:, h * X_HD:(h + 1) * X_HD]
        k_ref[:, h * X_HD:(h + 1) * X_HD] = _rms(kh, kg).astype(BF16)
    v_ref[...] = kv[:, X_INNER:].astype(BF16)


def _mem_kv(mem, norm_mem, xkv_w, xk_norm):
    m, d = mem.shape
    nl = norm_mem.shape[0]
    out = jax.ShapeDtypeStruct((nl, m, X_INNER), BF16)
    o_spec = pl.BlockSpec((None, m, X_INNER), lambda l: (l, 0, 0))
    return pl.pallas_call(
        _mem_kv_kernel, grid=(nl,), name="mem_kv",
        in_specs=[pl.BlockSpec((m, d), lambda l: (0, 0)),
                  pl.BlockSpec((None, 1, d), lambda l: (l, 0, 0)),
                  pl.BlockSpec((None, d, 2 * X_INNER), lambda l: (l, 0, 0)),
                  pl.BlockSpec((None, 1, X_HD), lambda l: (l, 0, 0))],
        out_specs=[o_spec, o_spec], out_shape=[out, out],
        compiler_params=_params(("parallel",), 2 * d * 2 * X_INNER * 2 + m * d * 4),
    )(mem, norm_mem, xkv_w, xk_norm)


def _mix_cross_kernel(*refs):
    *a_refs, wm_ref, x_ref, g_ref, wq_ref, qg_ref, k_ref, v_ref, wo_ref, o_ref = refs
    x = x_ref[...]
    k0 = 0
    for a_ref in a_refs:
        ka = a_ref.shape[1]
        x = x + _dot(a_ref[...], wm_ref[k0:k0 + ka, :])
        k0 += ka
    h = _rms(x, g_ref[...]).astype(BF16)
    q = _dot(h, wq_ref[...].astype(BF16))
    qg = qg_ref[...]
    outs = []
    for hh in range(X_HEADS):
        sl = slice(hh * X_HD, (hh + 1) * X_HD)
        qn = (_rms(q[:, sl], qg) * (X_HD ** -0.5 * LOG2E)).astype(BF16)
        sc = _dot_nt(qn, k_ref[:, sl])
        p = jnp.exp2(sc - jnp.max(sc, axis=-1, keepdims=True))
        inv = 1.0 / jnp.sum(p, axis=-1, keepdims=True)
        outs.append((_dot(p.astype(BF16), v_ref[:, sl]) * inv).astype(BF16))
    o = jnp.concatenate(outs, axis=-1)
    o_ref[...] = x + _dot(o, wo_ref[...].astype(BF16))


def _mix_cross_block(a_list, wm_all, wl, x, g_all, wq_all, qg_all, kn_all, v_all, wo_all, l, *, tm=512, name):
    s, d = x.shape
    tm = min(tm, s)
    m = kn_all.shape[1]
    kmix = wm_all.shape[1]
    nbytes = (4 * tm * d * 4 + 2 * tm * kmix * 2 + 2 * kmix * d * 2 + 4 * d * X_INNER * 4
              + 4 * m * X_INNER * 2 + 4 * tm * m * 4)
    layer = lambda shape: pl.BlockSpec((None,) + shape, lambda i: (l, 0, 0))
    return pl.pallas_call(
        _mix_cross_kernel, grid=(s // tm,), name=name,
        in_specs=[pl.BlockSpec((tm, a.shape[1]), lambda i: (i, 0)) for a in a_list]
        + [pl.BlockSpec((None, kmix, d), lambda i: (wl, 0, 0)),
           pl.BlockSpec((tm, d), lambda i: (i, 0)), layer((1, d)), layer((d, X_INNER)),
           layer((1, X_HD)), layer((m, X_INNER)), layer((m, X_INNER)), layer((X_INNER, d))],
        out_specs=pl.BlockSpec((tm, d), lambda i: (i, 0)),
        out_shape=jax.ShapeDtypeStruct((s, d), F32),
        compiler_params=_params(("parallel",), nbytes),
    )(*a_list, wm_all, x, g_all, wq_all, qg_all, kn_all, v_all, wo_all)


def _ffn_kernel(xh_ref, x_ref, g_ref, wg_ref, wu_ref, cw_ref, cb_ref, wo_ref, o_ref, h_ref, gt_ref):
    tm = x_ref.shape[0]

    @pl.when(pl.program_id(1) == 0)
    def _():
        g = g_ref[...]
        keep = (pl.program_id(0) > 0).astype(F32)
        h_ref[0:HALO, :] = _rms(xh_ref[...] * keep, g).astype(BF16)
        x = x_ref[...]
        h_ref[HALO:, :] = _rms(x, g).astype(BF16)
        o_ref[...] = x

    gt_ref[...] = _dot(h_ref[...], wg_ref[...])
    up = _dot(h_ref[HALO:, :], wu_ref[...])
    y = _causal_conv(cw_ref, gt_ref, HALO, tm, FFN_CONV) + cb_ref[...]
    act = (y * _sigmoid(y) * up).astype(BF16)
    o_ref[...] += _dot(act, wo_ref[...])


def _ffn_block(x, g_all, w_in_all, cw_all, cb_all, w_out_all, l, *, tm=1024, tf=512, name):
    s, d = x.shape
    tm = min(tm, s)
    nf = D_FF // tf
    nbytes = 4 * tm * d * 4 + (tm + HALO) * d * 2 + 6 * d * tf * 2 + 3 * (tm + HALO) * tf * 4
    return pl.pallas_call(
        _ffn_kernel, grid=(s // tm, nf), name=name,
        in_specs=[pl.BlockSpec((HALO, d), _halo_index(tm)),
                  pl.BlockSpec((tm, d), lambda i, j: (i, 0)),
                  pl.BlockSpec((None, 1, d), lambda i, j: (l, 0, 0)),
                  pl.BlockSpec((None, d, tf), lambda i, j: (l, 0, j)),
                  pl.BlockSpec((None, d, tf), lambda i, j: (l, 0, j + nf)),
                  pl.BlockSpec((None, FFN_CONV, tf), lambda i, j: (l, 0, j)),
                  pl.BlockSpec((None, 1, tf), lambda i, j: (l, 0, j)),
                  pl.BlockSpec((None, tf, d), lambda i, j: (l, j, 0))],
        out_specs=pl.BlockSpec((tm, d), lambda i, j: (i, 0)),
        out_shape=jax.ShapeDtypeStruct((s, d), F32),
        scratch_shapes=[pltpu.VMEM((tm + HALO, d), BF16), pltpu.VMEM((tm + HALO, tf), F32)],
        compiler_params=_params(("parallel", "arbitrary"), nbytes),
    )(x, x, g_all, w_in_all, w_in_all, cw_all, cb_all, w_out_all)


GDN_STEP_CHUNKS = 8
NEUMANN_DOUBLINGS = int(math.log2(GDN_CHUNK)) - 1


def _gdn_kernel(q_ref, k_ref, v_ref, z_ref, ba_ref, alog_ref, dt_ref, ng_ref, o_ref, st_ref):
    c = GDN_CHUNK
    rows = q_ref.shape[0]
    nchunks = rows // c

    @pl.when(pl.program_id(0) == 0)
    def _():
        st_ref[...] = jnp.zeros_like(st_ref)

    ba = ba_ref[...]
    lane = lax.broadcasted_iota(jnp.int32, ba.shape, 1)
    row_in_chunk = lax.broadcasted_iota(jnp.int32, ba.shape, 0) % c
    bg = jnp.where(lane < GDN_HEADS, _sigmoid(ba), -jnp.exp(alog_ref[...]) * _softplus(ba + dt_ref[...]))
    gcs = bg
    shift = 1
    while shift < c:
        gcs = gcs + jnp.where(row_in_chunk >= shift, pltpu.roll(gcs, shift, 0), 0.0)
        shift *= 2
    egcs = jnp.exp(gcs)

    ri = lax.broadcasted_iota(jnp.int32, (c, c), 0)
    ci = lax.broadcasted_iota(jnp.int32, (c, c), 1)
    causal = ri >= ci
    strict = ri > ci
    eye = (ri == ci).astype(F32)
    ng = ng_ref[...]

    heads = range(GDN_HEADS)
    chains = [(n, h) for n in range(nchunks) for h in heads]
    rsl = lambda n: slice(n * c, (n + 1) * c)
    hsl = lambda h: slice(h * GDN_DK, (h + 1) * GDN_DK)
    gct = [jnp.concatenate([gcs[rsl(n)], jnp.zeros((c, V7X_LANES), F32)], axis=0).T for n in range(nchunks)]

    kf, beta, gc_col, egc, g_last, decay, kb = {}, {}, {}, {}, {}, {}, {}
    for n, h in chains:
        kf[n, h] = k_ref[rsl(n), hsl(h)].astype(F32)
        beta[n, h] = bg[rsl(n), h:h + 1]
        gc_col[n, h] = gcs[rsl(n), GDN_HEADS + h:GDN_HEADS + h + 1]
        egc[n, h] = egcs[rsl(n), GDN_HEADS + h:GDN_HEADS + h + 1]
        gc_row = gct[n][GDN_HEADS + h:GDN_HEADS + h + 1, 0:c]
        g_last[n, h] = gc_row[:, c - 1:c]
        decay[n, h] = jnp.where(causal, jnp.exp(jnp.minimum(gc_col[n, h] - gc_row, 0.0)), 0.0)
        kb[n, h] = kf[n, h] * beta[n, h]
    qk = {ch: _dot_nt(jnp.concatenate([q_ref[rsl(ch[0]), hsl(ch[1])], kb[ch].astype(BF16)], axis=0),
                      k_ref[rsl(ch[0]), hsl(ch[1])]) for ch in chains}
    attn = {ch: (qk[ch][0:c] * decay[ch]).astype(BF16) for ch in chains}
    lmat = {ch: jnp.where(strict, qk[ch][c:2 * c] * decay[ch], 0.0) for ch in chains}
    pm = {ch: eye - lmat[ch] for ch in chains}
    mpow = {ch: lmat[ch].astype(BF16) for ch in chains}
    for _ in range(NEUMANN_DOUBLINGS):
        mpow = {ch: _dot(mpow[ch], mpow[ch]).astype(BF16) for ch in chains}
        pm = {ch: pm[ch] + _dot(pm[ch].astype(BF16), mpow[ch]) for ch in chains}
    uw = {ch: _dot(pm[ch].astype(BF16),
                   jnp.concatenate([v_ref[rsl(ch[0]), hsl(ch[1])].astype(F32) * beta[ch], kb[ch] * egc[ch]],
                                   axis=1).astype(BF16)) for ch in chains}
    wq = {ch: jnp.concatenate([uw[ch][:, GDN_DV:],
                               q_ref[rsl(ch[0]), hsl(ch[1])].astype(F32) * egc[ch]], axis=0).astype(BF16)
          for ch in chains}
    k_dec = {ch: (kf[ch] * jnp.exp(g_last[ch] - gc_col[ch])).astype(BF16) for ch in chains}

    state = [st_ref[h] for h in heads]
    for n in range(nchunks):
        ws = [_dot(wq[n, h], state[h].astype(BF16)) for h in heads]
        v_new = [(uw[n, h][:, 0:GDN_DV] - ws[h][0:c]).astype(BF16) for h in heads]
        state = [state[h] * jnp.exp(g_last[n, h]) + _dot_tn(k_dec[n, h], v_new[h]) for h in heads]
        o = [ws[h][c:2 * c] + _dot(attn[n, h], v_new[h]) for h in heads]
        for h in heads:
            o_ref[rsl(n), hsl(h)] = (_rms(o[h], ng) * z_ref[rsl(n), hsl(h)].astype(F32)).astype(o_ref.dtype)
    for h in heads:
        st_ref[h] = state[h]


def _gdn_mixer(p, ba, alog_all, dt_all, ng_all, e, *, name):
    s = p.shape[0]
    rows = GDN_STEP_CHUNKS * GDN_CHUNK
    col = lambda b: pl.BlockSpec((rows, GDN_V), lambda i: (i, b))
    layer = lambda shape: pl.BlockSpec((None,) + shape, lambda i: (e, 0, 0))
    return pl.pallas_call(
        _gdn_kernel, grid=(s // rows,), name=name,
        in_specs=[col(0), col(1), col(2), col(3),
                  pl.BlockSpec((rows, V7X_LANES), lambda i: (i, 0)),
                  layer((1, V7X_LANES)), layer((1, V7X_LANES)), layer((1, GDN_DV))],
        out_specs=pl.BlockSpec((rows, GDN_V), lambda i: (i, 0)),
        out_shape=jax.ShapeDtypeStruct((s, GDN_V), BF16),
        scratch_shapes=[pltpu.VMEM((GDN_HEADS, GDN_DK, GDN_DV), F32)],
        compiler_params=_params(("arbitrary",), 16 * rows * GDN_V * 4),
    )(p, p, p, p, ba, alog_all, dt_all, ng_all)


def _lru_kernel(xc_ref, gate_ref, wr_ref, br_ref, wi_ref, bi_ref, lam_ref, o_ref, a_ref, b_ref, carry_ref):
    tt = xc_ref.shape[0]

    @pl.when(pl.program_id(0) == 0)
    def _():
        carry_ref[...] = jnp.zeros_like(carry_ref)

    for blk in range(LRU_BLOCKS):
        sl = slice(blk * LRU_BLOCK, (blk + 1) * LRU_BLOCK)
        xb = xc_ref[:, sl]
        r = _sigmoid_tanh(_dot(xb, wr_ref[blk]) + br_ref[:, sl])
        i = _sigmoid_tanh(_dot(xb, wi_ref[blk]) + bi_ref[:, sl])
        log_a = (-LRU_C) * r * _softplus(-lam_ref[:, sl])
        a = jnp.exp(log_a)
        one_minus_a2 = -jnp.tanh(log_a) * (1.0 + a * a)
        a_ref[:, sl] = a
        b_ref[:, sl] = jnp.sqrt(one_minus_a2) * (i * xb.astype(F32))

    row = lax.broadcasted_iota(jnp.int32, (V7X_SUBLANES, LRU_WIDTH), 0)

    def group(gi, hprev):
        r0 = pl.multiple_of(gi * V7X_SUBLANES, V7X_SUBLANES)
        a = a_ref[pl.ds(r0, V7X_SUBLANES), :]
        b = b_ref[pl.ds(r0, V7X_SUBLANES), :]
        shift = 1
        while shift < V7X_SUBLANES:
            keep = row >= shift
            b = b + a * jnp.where(keep, pltpu.roll(b, shift, 0), 0.0)
            a = a * jnp.where(keep, pltpu.roll(a, shift, 0), 1.0)
            shift *= 2
        hcur = b + a * hprev
        b_ref[pl.ds(r0, V7X_SUBLANES), :] = hcur
        return jnp.broadcast_to(hcur[V7X_SUBLANES - 1:V7X_SUBLANES, :], hcur.shape)

    carry_ref[...] = lax.fori_loop(0, tt // V7X_SUBLANES, group, carry_ref[...])
    o_ref[...] = (b_ref[...] * gate_ref[...].astype(F32)).astype(o_ref.dtype)


def _lru_mixer(p, wr_all, br_all, wi_all, bi_all, lam_all, e, *, tt=512, name):
    s = p.shape[0]
    tt = min(tt, s)
    vec = pl.BlockSpec((None, 1, LRU_WIDTH), lambda i: (e, 0, 0))
    mat = pl.BlockSpec((None, LRU_BLOCKS, LRU_BLOCK, LRU_BLOCK), lambda i: (e, 0, 0, 0))
    return pl.pallas_call(
        _lru_kernel, grid=(s // tt,), name=name,
        in_specs=[pl.BlockSpec((tt, LRU_WIDTH), lambda i: (i, 4)),
                  pl.BlockSpec((tt, LRU_WIDTH), lambda i: (i, 5)),
                  mat, vec, mat, vec, vec],
        out_specs=pl.BlockSpec((tt, LRU_WIDTH), lambda i: (i, 0)),
        out_shape=jax.ShapeDtypeStruct((s, LRU_WIDTH), BF16),
        scratch_shapes=[pltpu.VMEM((tt, LRU_WIDTH), F32),
                        pltpu.VMEM((tt, LRU_WIDTH), F32),
                        pltpu.VMEM((V7X_SUBLANES, LRU_WIDTH), F32)],
        compiler_params=_params(("arbitrary",), 10 * tt * LRU_WIDTH * 4),
    )(p, p, wr_all, br_all, wi_all, bi_all, lam_all)


def _rope_kernel(pos_ref, inv_ref, cos_ref, sin_ref):
    ang = pos_ref[...].astype(F32) * inv_ref[...]
    cos_ref[...] = jnp.cos(ang)
    sin_ref[...] = jnp.sin(ang)


def _rope_tables(pos_col, inv_lanes, *, tm=1024):
    s = pos_col.shape[0]
    tm = min(tm, s)
    out = jax.ShapeDtypeStruct((s, V7X_LANES), F32)
    o_spec = pl.BlockSpec((tm, V7X_LANES), lambda i: (i, 0))
    return pl.pallas_call(
        _rope_kernel, grid=(s // tm,), name="rope_tables",
        in_specs=[pl.BlockSpec((tm, 1), lambda i: (i, 0)), pl.BlockSpec((1, V7X_LANES), lambda i: (0, 0))],
        out_specs=[o_spec, o_spec], out_shape=[out, out],
        compiler_params=_params(("parallel",), 8 * tm * V7X_LANES * 4),
    )(pos_col, inv_lanes)


SWA_PAIR = 2 * V7X_LANES
SWA_Q_PAIRS = SWA_Q // SWA_PAIR
SWA_KV_PAIRS = SWA_KV // SWA_PAIR
SWA_GROUPS_PER_PAIR = V7X_LANES // SWA_HALF


def _swa_kernel(sink_ref, qkv_ref, cos_ref, sin_ref, qg_ref, kg_ref, seg_ref, o_ref, k_ref, v_ref):
    blk = SWA_BLOCK
    first = pl.program_id(0) == 0

    @pl.when(first)
    def _():
        k_ref[0:blk, :] = jnp.zeros((blk, SWA_KV), BF16)
        v_ref[0:blk, :] = jnp.zeros((blk, SWA_KV), BF16)

    @pl.when(jnp.logical_not(first))
    def _():
        k_ref[0:blk, :] = k_ref[blk:2 * blk, :]
        v_ref[0:blk, :] = v_ref[blk:2 * blk, :]

    cos = cos_ref[...]
    sin = sin_ref[...]
    seg = seg_ref[...]

    def norm_rope(col0, gain_ref):
        a = qkv_ref[:, col0:col0 + V7X_LANES]
        b = qkv_ref[:, col0 + V7X_LANES:col0 + SWA_PAIR]
        sq = a * a + b * b
        hi = sq.astype(BF16)
        lo = (sq - hi.astype(F32)).astype(BF16)
        ss = _dot(hi, seg) + _dot(lo, seg)
        scale = lax.rsqrt(ss * (1.0 / SWA_HD) + EPS)
        a = a * scale * gain_ref[:, 0:V7X_LANES]
        b = b * scale * gain_ref[:, V7X_LANES:SWA_PAIR]
        return a * cos - b * sin, b * cos + a * sin

    for u in range(SWA_KV_PAIRS):
        ka, kb = norm_rope(SWA_Q + u * SWA_PAIR, kg_ref)
        k_ref[blk:2 * blk, u * SWA_PAIR:u * SWA_PAIR + V7X_LANES] = ka.astype(BF16)
        k_ref[blk:2 * blk, u * SWA_PAIR + V7X_LANES:(u + 1) * SWA_PAIR] = kb.astype(BF16)
    v_ref[blk:2 * blk, :] = qkv_ref[:, SWA_Q + SWA_KV:SWA_Q + 2 * SWA_KV].astype(BF16)

    qi = lax.broadcasted_iota(jnp.int32, (blk, 2 * blk), 0)
    kj = lax.broadcasted_iota(jnp.int32, (blk, 2 * blk), 1)
    lo_key = jnp.where(first, blk, 0)
    valid = (kj > qi) & (kj <= qi + SWA_WINDOW) & (kj >= lo_key)
    lane_group = lax.broadcasted_iota(jnp.int32, (blk, SWA_PAIR), 1) % V7X_LANES // SWA_HALF

    pairs = range(SWA_Q_PAIRS)
    groups = range(SWA_GROUPS_PER_PAIR)
    kv_pair = lambda t: t // SWA_GROUPS_PER_PAIR
    kv_cols = lambda t: slice(kv_pair(t) * SWA_PAIR, (kv_pair(t) + 1) * SWA_PAIR)
    qs = []
    for t in pairs:
        qa, qb = norm_rope(t * SWA_PAIR, qg_ref)
        qp = jnp.concatenate([qa, qb], axis=1) * (SWA_HD ** -0.5 * LOG2E)
        qs.append(jnp.concatenate([jnp.where(lane_group == gl, qp, 0.0) for gl in groups],
                                  axis=0).astype(BF16))
    sc_all = [_dot_nt(qs[t], k_ref[:, kv_cols(t)]) for t in pairs]
    ps, invs = [], []
    for t in pairs:
        pt, it = [], []
        for gl in groups:
            head = SWA_GROUP * (SWA_GROUPS_PER_PAIR * kv_pair(t) + gl) + t % SWA_GROUPS_PER_PAIR
            sink = sink_ref[head] * LOG2E
            sc = jnp.where(valid, sc_all[t][gl * blk:(gl + 1) * blk], -jnp.inf)
            m = jnp.maximum(jnp.max(sc, axis=-1, keepdims=True), sink)
            p = jnp.exp2(sc - m)
            it.append(1.0 / (jnp.sum(p, axis=-1, keepdims=True) + jnp.exp2(sink - m)))
            pt.append(p.astype(BF16))
        ps.append(jnp.concatenate(pt, axis=0))
        invs.append(it)
    pv = [_dot(ps[t], v_ref[:, kv_cols(t)]) for t in pairs]
    for t in pairs:
        out = jnp.zeros((blk, SWA_PAIR), F32)
        for gl in groups:
            out = jnp.where(lane_group == gl, pv[t][gl * blk:(gl + 1) * blk] * invs[t][gl], out)
        o_ref[:, t * SWA_PAIR:(t + 1) * SWA_PAIR] = out.astype(o_ref.dtype)


def _swa_mixer(qkv, sinks_all, cos, sin, qg_all, kg_all, seg, o, *, name):
    s, n = qkv.shape
    blk = SWA_BLOCK
    layer = lambda shape: pl.BlockSpec((None,) + shape, lambda i: (o, 0, 0))
    return pl.pallas_call(
        _swa_kernel, grid=(s // blk,), name=name,
        in_specs=[pl.BlockSpec(memory_space=pltpu.SMEM),
                  pl.BlockSpec((blk, n), lambda i: (i, 0)),
                  pl.BlockSpec((blk, V7X_LANES), lambda i: (i, 0)),
                  pl.BlockSpec((blk, V7X_LANES), lambda i: (i, 0)),
                  layer((1, SWA_PAIR)), layer((1, SWA_PAIR)),
                  pl.BlockSpec((V7X_LANES, V7X_LANES), lambda i: (0, 0))],
        out_specs=pl.BlockSpec((blk, SWA_Q), lambda i: (i, 0)),
        out_shape=jax.ShapeDtypeStruct((s, SWA_Q), BF16),
        scratch_shapes=[pltpu.VMEM((2 * blk, SWA_KV), BF16), pltpu.VMEM((2 * blk, SWA_KV), BF16)],
        compiler_params=_params(("arbitrary",), 8 * blk * n * 4),
    )(sinks_all[o], qkv, cos, sin, qg_all, kg_all, seg)


def _swa_q_src(t, half, gl):
    head = SWA_GROUPS_PER_PAIR * SWA_GROUP * (t // SWA_GROUPS_PER_PAIR) + SWA_GROUP * gl + t % SWA_GROUPS_PER_PAIR
    return head * SWA_HD + half * SWA_HALF


def _swa_kv_src(u, half, gl):
    return (SWA_GROUPS_PER_PAIR * u + gl) * SWA_HD + half * SWA_HALF


def _swa_in_prep_kernel(w_ref, o_ref):
    groups = range(SWA_GROUPS_PER_PAIR)

    def tile(srcs):
        return jnp.concatenate([w_ref[:, s:s + SWA_HALF] for s in srcs], axis=1).astype(BF16)

    for t in range(SWA_Q_PAIRS):
        for half in range(2):
            c0 = t * SWA_PAIR + half * V7X_LANES
            o_ref[:, c0:c0 + V7X_LANES] = tile([_swa_q_src(t, half, gl) for gl in groups])
    for base in (SWA_Q, SWA_Q + SWA_KV):
        for u in range(SWA_KV_PAIRS):
            for half in range(2):
                c0 = base + u * SWA_PAIR + half * V7X_LANES
                o_ref[:, c0:c0 + V7X_LANES] = tile([base + _swa_kv_src(u, half, gl) for gl in groups])


def _swa_in_prep(w, *, tr=256):
    n_layers, d, n = w.shape
    spec = pl.BlockSpec((None, tr, n), lambda e, i: (e, i, 0))
    return pl.pallas_call(
        _swa_in_prep_kernel, grid=(n_layers, d // tr), name="swa_in_prep",
        in_specs=[spec], out_specs=spec, out_shape=jax.ShapeDtypeStruct(w.shape, BF16),
        compiler_params=_params(("parallel", "parallel"), 2 * tr * n * 6),
    )(w)


def _swa_out_prep_kernel(w_ref, o_ref):
    for t in range(SWA_Q_PAIRS):
        for half in range(2):
            for gl in range(SWA_GROUPS_PER_PAIR):
                new = t * SWA_PAIR + half * V7X_LANES + gl * SWA_HALF
                src = _swa_q_src(t, half, gl)
                o_ref[new:new + SWA_HALF, :] = w_ref[src:src + SWA_HALF, :].astype(BF16)


def _swa_out_prep(w, *, tc=1024):
    n_layers, k, d = w.shape
    spec = pl.BlockSpec((None, k, tc), lambda e, j: (e, 0, j))
    return pl.pallas_call(
        _swa_out_prep_kernel, grid=(n_layers, d // tc), name="swa_out_prep",
        in_specs=[spec], out_specs=spec, out_shape=jax.ShapeDtypeStruct(w.shape, BF16),
        compiler_params=_params(("parallel", "parallel"), 2 * k * tc * 6),
    )(w)


def _gain_lanes(g):
    n = g.shape[0]
    g = jnp.broadcast_to(g.reshape(n, 2, 1, SWA_HALF), (n, 2, SWA_GROUPS_PER_PAIR, SWA_HALF))
    return g.reshape(n, 1, SWA_PAIR).astype(F32)


def _lane_rows(vals, offset):
    n, m = vals.shape
    return jnp.pad(vals.astype(F32), ((0, 0), (offset, V7X_LANES - offset - m))).reshape(n, 1, V7X_LANES)


def kernel(x, mem, positions, norm_mix, norm_cross, norm_mem, norm_ffn, xq_w, xkv_w, xo_w, xq_norm, xk_norm, ffn_in_w, ffn_conv_w, ffn_conv_b, ffn_out_w, hyb_in_w, hyb_out_w, gdn_conv_w, gdn_a_log, gdn_dt_bias, gdn_norm, lru_conv_w, lru_conv_b, lru_wr, lru_br, lru_wi, lru_bi, lru_lambda, swa_in_w, swa_out_w, swa_q_norm, swa_k_norm, swa_sinks):
    bsz, s, d = x.shape
    assert bsz == 1 and d == D_MODEL and s % 1024 == 0
    xs = x[0]
    rows3 = lambda v: v.reshape(v.shape[0], 1, v.shape[1]).astype(F32)

    norm_mix3, norm_cross3, norm_ffn3 = rows3(norm_mix), rows3(norm_cross), rows3(norm_ffn)
    hyb_t = hyb_in_w.transpose(0, 2, 1).astype(BF16)
    hyb_out_b = hyb_out_w.astype(BF16)
    alog3, dt3 = _lane_rows(gdn_a_log, GDN_HEADS), _lane_rows(gdn_dt_bias, GDN_HEADS)
    gdn_norm3 = rows3(gdn_norm)
    lru_cb3, lru_br3, lru_bi3, lru_lam3 = rows3(lru_conv_b), rows3(lru_br), rows3(lru_bi), rows3(lru_lambda)
    lru_wr_b, lru_wi_b = lru_wr.astype(BF16), lru_wi.astype(BF16)
    swa_in_b = _swa_in_prep(swa_in_w)
    swa_out_b = _swa_out_prep(swa_out_w)
    swa_qg, swa_kg = _gain_lanes(swa_q_norm), _gain_lanes(swa_k_norm)
    lanes = jnp.arange(V7X_LANES)
    seg = (lanes[:, None] // SWA_HALF == lanes[None, :] // SWA_HALF).astype(BF16)
    xq_norm3 = rows3(xq_norm)
    ffn_in_b, ffn_out_b = ffn_in_w.astype(BF16), ffn_out_w.astype(BF16)
    ffn_cb3 = rows3(ffn_conv_b)

    inv = 1.0 / (ROPE_THETA ** (jnp.arange(0, SWA_HD, 2, dtype=F32) / SWA_HD))
    inv_lanes = jnp.tile(inv, SWA_GROUPS_PER_PAIR).reshape(1, V7X_LANES)
    cos, sin = _rope_tables(positions.reshape(s, 1), inv_lanes)
    kn_all, v_all = _mem_kv(mem[0], rows3(norm_mem), xkv_w, rows3(xk_norm))

    for l in range(DEPTH):
        if l % 2 == 0:
            e = l // 2
            p, ba = _hyb_in(xs, norm_mix3, hyb_t, gdn_conv_w, lru_conv_w, lru_cb3, l, e,
                            name=f"hyb_in_{l}")
            oa = _gdn_mixer(p, ba, alog3, dt3, gdn_norm3, e, name=f"gdn_{l}")
            ob = _lru_mixer(p, lru_wr_b, lru_br3, lru_wi_b, lru_bi3, lru_lam3, e, name=f"lru_{l}")
            mixed, w_mix, wl = [oa, ob], hyb_out_b, e
        else:
            o = l // 2
            qkv = _norm_matmul(xs, norm_mix3, swa_in_b, l, o, name=f"swa_in_{l}")
            att = _swa_mixer(qkv, swa_sinks.astype(F32), cos, sin, swa_qg, swa_kg, seg, o, name=f"swa_{l}")
            mixed, w_mix, wl = [att], swa_out_b, o
        xs = _mix_cross_block(mixed, w_mix, wl, xs, norm_cross3, xq_w, xq_norm3, kn_all, v_all, xo_w, l,
                              name=f"mix_cross_{l}")
        xs = _ffn_block(xs, norm_ffn3, ffn_in_b, ffn_conv_w, ffn_cb3, ffn_out_b, l, name=f"ffn_{l}")
    return xs[None]
```

```python
import math

import jax
import jax.numpy as jnp
from jax import lax
from jax.experimental import pallas as pl
from jax.experimental.pallas import tpu as pltpu

F32 = jnp.float32
BF16 = jnp.bfloat16

D_MODEL = 2048
DEPTH = 4
MEM_LEN = 256
EPS = 1e-6
GDN_HEADS = 8
GDN_DK = 128
GDN_DV = 128
GDN_CONV = 4
GDN_CHUNK = 64
GDN_QK = GDN_HEADS * GDN_DK
GDN_V = GDN_HEADS * GDN_DV
GDN_QKV = 2 * GDN_QK + GDN_V
LRU_WIDTH = D_MODEL // 2
LRU_BLOCKS = 8
LRU_BLOCK = LRU_WIDTH // LRU_BLOCKS
LRU_CONV = 4
LRU_C = 8.0
SWA_HEADS = 32
SWA_KV_HEADS = 8
SWA_GROUP = SWA_HEADS // SWA_KV_HEADS
SWA_HD = 64
SWA_HALF = SWA_HD // 2
SWA_WINDOW = 128
SWA_BLOCK = 128
SWA_Q = SWA_HEADS * SWA_HD
SWA_KV = SWA_KV_HEADS * SWA_HD
ROPE_THETA = 10000.0
X_HEADS = 4
X_HD = 128
X_INNER = X_HEADS * X_HD
D_FF = 5632
FFN_CONV = 3
LOG2E = math.log2(math.e)

V7X_LANES = 128
V7X_SUBLANES = 8
V7X_BF16_ROWS = 16
V7X_VMEM_BYTES = 64 * 1024 * 1024

HALO = V7X_BF16_ROWS


def _vmem_limit(nbytes):
    return int(min(max(2 * nbytes, 16 * 1024 * 1024), V7X_VMEM_BYTES - 8 * 1024 * 1024))


def _params(semantics, nbytes):
    return pltpu.CompilerParams(dimension_semantics=semantics, vmem_limit_bytes=_vmem_limit(nbytes))


def _rms(x, g):
    return x * lax.rsqrt(jnp.mean(x * x, axis=-1, keepdims=True) + EPS) * g


def _sigmoid(x):
    return 1.0 / (1.0 + jnp.exp(-x))


def _sigmoid_tanh(x):
    return 0.5 * jnp.tanh(0.5 * x) + 0.5


def _softplus(x):
    return jnp.maximum(x, 0.0) + jnp.log1p(jnp.exp(-jnp.abs(x)))


def _gelu_tanh(x):
    return 0.5 * x * (1.0 + jnp.tanh(math.sqrt(2.0 / math.pi) * (x + 0.044715 * (x * x * x))))


def _dot(a, b):
    return jnp.dot(a, b, preferred_element_type=F32)


def _dot_nt(a, b):
    return lax.dot_general(a, b, (((1,), (1,)), ((), ())), preferred_element_type=F32)


def _dot_tn(a, b):
    return lax.dot_general(a, b, (((0,), (0,)), ((), ())), preferred_element_type=F32)


def _halo_index(tm):
    per_tile = tm // HALO
    return lambda i, j: (jnp.maximum(i * per_tile - 1, 0), 0)


def _causal_conv(cw_ref, src_ref, row0, rows, width):
    win = src_ref[row0 - V7X_SUBLANES:row0 + rows, :]
    y = cw_ref[width - 1:width, :] * win[V7X_SUBLANES:, :]
    for j in range(width - 1):
        back = width - 1 - j
        y = y + cw_ref[j:j + 1, :] * pltpu.roll(win, back, 0)[V7X_SUBLANES:, :]
    return y


def _norm_matmul_kernel(x_ref, g_ref, w_ref, o_ref, h_ref):
    @pl.when(pl.program_id(1) == 0)
    def _():
        h_ref[...] = _rms(x_ref[...], g_ref[...]).astype(BF16)

    o_ref[...] = _dot(h_ref[...], w_ref[...]).astype(o_ref.dtype)


def _norm_matmul(x, g_all, w_all, l, wl, *, tm=1024, tn=1536, name):
    s, d = x.shape
    n = w_all.shape[-1]
    tm, tn = min(tm, s), min(tn, n)
    nbytes = 2 * tm * d * 4 + tm * d * 2 + 2 * d * tn * 2 + 2 * tm * tn * 4
    return pl.pallas_call(
        _norm_matmul_kernel, grid=(s // tm, n // tn), name=name,
        in_specs=[pl.BlockSpec((tm, d), lambda i, j: (i, 0)),
                  pl.BlockSpec((None, 1, d), lambda i, j: (l, 0, 0)),
                  pl.BlockSpec((None, d, tn), lambda i, j: (wl, 0, j))],
        out_specs=pl.BlockSpec((tm, tn), lambda i, j: (i, j)),
        out_shape=jax.ShapeDtypeStruct((s, n), F32),
        scratch_shapes=[pltpu.VMEM((tm, d), BF16)],
        compiler_params=_params(("parallel", "arbitrary"), nbytes),
    )(x, g_all, w_all)


HYB_TN = 1024
HYB_BLOCKS = 6
HYB_BA0 = GDN_QKV + GDN_V
HYB_LRU0 = HYB_BA0 + 2 * GDN_HEADS
HYB_EPI_ROWS = 256


HYB_PIECE = 4 * V7X_LANES


def _hyb_in_kernel(xh_ref, x_ref, g_ref, wg_ref, wl_ref, wba_ref, gcw_ref, lcw_ref, lcb_ref,
                   p_ref, ba_ref, h_ref, *pe_refs):
    tm = x_ref.shape[0]
    j = pl.program_id(1)
    tiles = range(0, tm, HYB_EPI_ROWS)
    npieces = len(pe_refs)

    @pl.when(j == 0)
    def _():
        g = g_ref[...]
        keep = (pl.program_id(0) > 0).astype(F32)
        h_ref[0:HALO, :] = _rms(xh_ref[...] * keep, g).astype(BF16)
        hb = _rms(x_ref[...], g).astype(BF16)
        h_ref[HALO:, :] = hb
        ba_ref[...] = _dot_nt(hb, wba_ref[...])

    def matmul(blk, piece):
        w_ref = wg_ref if blk < 4 else wl_ref.at[0]
        pe_refs[piece][...] = _dot_nt(h_ref[...], w_ref[piece * HYB_PIECE:(piece + 1) * HYB_PIECE, :])

    def epilogue(blk, piece):
        src = pe_refs[piece]
        cols = slice(piece * HYB_PIECE, (piece + 1) * HYB_PIECE)
        for r0 in tiles:
            rows = slice(r0, r0 + HYB_EPI_ROWS)
            if blk <= 2:
                y = _causal_conv(gcw_ref.at[:, cols], src, HALO + r0, HYB_EPI_ROWS, GDN_CONV)
                y = y * _sigmoid(y)
                if blk == 2:
                    p_ref[rows, cols] = y.astype(p_ref.dtype)
                    continue
                qscale = GDN_DK ** -0.5 if blk == 0 else 1.0
                for h in range(HYB_PIECE // GDN_DK):
                    yh = y[:, h * GDN_DK:(h + 1) * GDN_DK]
                    inv = lax.rsqrt(jnp.sum(yh * yh, axis=-1, keepdims=True) + EPS) * qscale
                    c0 = piece * HYB_PIECE + h * GDN_DK
                    p_ref[rows, c0:c0 + GDN_DK] = (yh * inv).astype(p_ref.dtype)
            elif blk == 3:
                z = src[HALO + r0:HALO + r0 + HYB_EPI_ROWS, :]
                p_ref[rows, cols] = (z * _sigmoid(z)).astype(p_ref.dtype)
            elif blk == 4:
                xc = _causal_conv(lcw_ref.at[:, cols], src, HALO + r0, HYB_EPI_ROWS, LRU_CONV) + lcb_ref[:, cols]
                p_ref[rows, cols] = xc.astype(p_ref.dtype)
            else:
                gate = src[HALO + r0:HALO + r0 + HYB_EPI_ROWS, :]
                p_ref[rows, cols] = _gelu_tanh(gate).astype(p_ref.dtype)

    for blk in range(HYB_BLOCKS):
        @pl.when(j == blk)
        def _(blk=blk):
            for piece in range(npieces):
                matmul(blk, piece)
            for piece in range(npieces):
                epilogue(blk, piece)


def _hyb_in(x, g_all, w_t, gdn_cw, lru_cw, lru_cb, l, e, *, tm=1024, name):
    s, d = x.shape
    tm, tn = min(tm, s), HYB_TN
    nbytes = (2 * tm * d * 4 + (tm + HALO) * d * 2 + 4 * d * tn * 2 + (tm + HALO) * tn * 4
              + 2 * tm * tn * 2 + 8 * HYB_EPI_ROWS * HYB_PIECE * 4)
    return pl.pallas_call(
        _hyb_in_kernel, grid=(s // tm, HYB_BLOCKS), name=name,
        in_specs=[pl.BlockSpec((HALO, d), _halo_index(tm)),
                  pl.BlockSpec((tm, d), lambda i, j: (i, 0)),
                  pl.BlockSpec((None, 1, d), lambda i, j: (l, 0, 0)),
                  pl.BlockSpec((None, tn, d), lambda i, j: (e, jnp.minimum(j, 3), 0)),
                  pl.BlockSpec((pl.Element(1), pl.Element(tn), pl.Element(d)),
                               lambda i, j: (e, pl.multiple_of(HYB_LRU0 + tn * jnp.maximum(j - 4, 0),
                                                               V7X_BF16_ROWS), 0)),
                  pl.BlockSpec((None, V7X_LANES, d), lambda i, j: (e, HYB_BA0 // V7X_LANES, 0)),
                  pl.BlockSpec((None, GDN_CONV, tn), lambda i, j: (e, 0, jnp.minimum(j, 2))),
                  pl.BlockSpec((None, LRU_CONV, LRU_WIDTH), lambda i, j: (e, 0, 0)),
                  pl.BlockSpec((None, 1, LRU_WIDTH), lambda i, j: (e, 0, 0))],
        out_specs=[pl.BlockSpec((tm, tn), lambda i, j: (i, j)),
                   pl.BlockSpec((tm, V7X_LANES), lambda i, j: (i, 0))],
        out_shape=[jax.ShapeDtypeStruct((s, HYB_BLOCKS * tn), BF16),
                   jax.ShapeDtypeStruct((s, V7X_LANES), F32)],
        scratch_shapes=[pltpu.VMEM((tm + HALO, d), BF16)]
        + [pltpu.VMEM((tm + HALO, HYB_PIECE), F32) for _ in range(tn // HYB_PIECE)],
        compiler_params=_params(("parallel", "arbitrary"), nbytes),
    )(x, x, g_all, w_t, w_t, w_t, gdn_cw, lru_cw, lru_cb)


def _mem_kv_kernel(mem_ref, g_ref, w_ref, kg_ref, k_ref, v_ref):
    mn = _rms(mem_ref[...], g_ref[...]).astype(BF16)
    kv = _dot(mn, w_ref[...].astype(BF16))
    kg = kg_ref[...]
    for h in range(X_HEADS):
        kh = kv[:, h * X_HD:(h + 1) * X_HD]
        k_ref[:, h * X_HD:(h + 1) * X_HD] = _rms(kh, kg).astype(BF16)
    v_ref[...] = kv[:, X_INNER:].astype(BF16)


def _mem_kv(mem, norm_mem, xkv_w, xk_norm):
    m, d = mem.shape
    nl = norm_mem.shape[0]
    out = jax.ShapeDtypeStruct((nl, m, X_INNER), BF16)
    o_spec = pl.BlockSpec((None, m, X_INNER), lambda l: (l, 0, 0))
    return pl.pallas_call(
        _mem_kv_kernel, grid=(nl,), name="mem_kv",
        in_specs=[pl.BlockSpec((m, d), lambda l: (0, 0)),
                  pl.BlockSpec((None, 1, d), lambda l: (l, 0, 0)),
                  pl.BlockSpec((None, d, 2 * X_INNER), lambda l: (l, 0, 0)),
                  pl.BlockSpec((None, 1, X_HD), lambda l: (l, 0, 0))],
        out_specs=[o_spec, o_spec], out_shape=[out, out],
        compiler_params=_params(("parallel",), 2 * d * 2 * X_INNER * 2 + m * d * 4),
    )(mem, norm_mem, xkv_w, xk_norm)


def _mix_cross_kernel(*refs):
    *a_refs, wm_ref, x_ref, g_ref, wq_ref, qg_ref, k_ref, v_ref, wo_ref, o_ref = refs
    x = x_ref[...]
    k0 = 0
    for a_ref in a_refs:
        ka = a_ref.shape[1]
        x = x + _dot(a_ref[...], wm_ref[k0:k0 + ka, :])
        k0 += ka
    h = _rms(x, g_ref[...]).astype(BF16)
    q = _dot(h, wq_ref[...].astype(BF16))
    qg = qg_ref[...]
    heads = range(X_HEADS)
    hsl = lambda hh: slice(hh * X_HD, (hh + 1) * X_HD)
    qn = [(_rms(q[:, hsl(hh)], qg) * (X_HD ** -0.5 * LOG2E)).astype(BF16) for hh in heads]
    sc = [_dot_nt(qn[hh], k_ref[:, hsl(hh)]) for hh in heads]
    p = [jnp.exp2(sc[hh] - jnp.max(sc[hh], axis=-1, keepdims=True)) for hh in heads]
    inv = [1.0 / jnp.sum(p[hh], axis=-1, keepdims=True) for hh in heads]
    pv = [_dot(p[hh].astype(BF16), v_ref[:, hsl(hh)]) for hh in heads]
    o = jnp.concatenate([(pv[hh] * inv[hh]).astype(BF16) for hh in heads], axis=-1)
    o_ref[...] = x + _dot(o, wo_ref[...].astype(BF16))


def _mix_cross_block(a_list, wm_all, wl, x, g_all, wq_all, qg_all, kn_all, v_all, wo_all, l, *, tm=512, name):
    s, d = x.shape
    tm = min(tm, s)
    m = kn_all.shape[1]
    kmix = wm_all.shape[1]
    nbytes = (4 * tm * d * 4 + 2 * tm * kmix * 2 + 2 * kmix * d * 2 + 4 * d * X_INNER * 4
              + 4 * m * X_INNER * 2 + 4 * tm * m * 4)
    layer = lambda shape: pl.BlockSpec((None,) + shape, lambda i: (l, 0, 0))
    return pl.pallas_call(
        _mix_cross_kernel, grid=(s // tm,), name=name,
        in_specs=[pl.BlockSpec((tm, a.shape[1]), lambda i: (i, 0)) for a in a_list]
        + [pl.BlockSpec((None, kmix, d), lambda i: (wl, 0, 0)),
           pl.BlockSpec((tm, d), lambda i: (i, 0)), layer((1, d)), layer((d, X_INNER)),
           layer((1, X_HD)), layer((m, X_INNER)), layer((m, X_INNER)), layer((X_INNER, d))],
        out_specs=pl.BlockSpec((tm, d), lambda i: (i, 0)),
        out_shape=jax.ShapeDtypeStruct((s, d), F32),
        compiler_params=_params(("parallel",), nbytes),
    )(*a_list, wm_all, x, g_all, wq_all, qg_all, kn_all, v_all, wo_all)


def _ffn_kernel(xh_ref, x_ref, g_ref, wg_ref, wu_ref, cw_ref, cb_ref, wo_ref, o_ref, h_ref, gt_ref):
    tm = x_ref.shape[0]

    @pl.when(pl.program_id(1) == 0)
    def _():
        g = g_ref[...]
        keep = (pl.program_id(0) > 0).astype(F32)
        h_ref[0:HALO, :] = _rms(xh_ref[...] * keep, g).astype(BF16)
        x = x_ref[...]
        h_ref[HALO:, :] = _rms(x, g).astype(BF16)
        o_ref[...] = x

    gt_ref[...] = _dot(h_ref[...], wg_ref[...])
    up = _dot(h_ref[HALO:, :], wu_ref[...])
    y = _causal_conv(cw_ref, gt_ref, HALO, tm, FFN_CONV) + cb_ref[...]
    act = (y * _sigmoid(y) * up).astype(BF16)
    o_ref[...] += _dot(act, wo_ref[...])


def _ffn_block(x, g_all, w_in_all, cw_all, cb_all, w_out_all, l, *, tm=1024, tf=512, name):
    s, d = x.shape
    tm = min(tm, s)
    nf = D_FF // tf
    nbytes = 4 * tm * d * 4 + (tm + HALO) * d * 2 + 6 * d * tf * 2 + 3 * (tm + HALO) * tf * 4
    return pl.pallas_call(
        _ffn_kernel, grid=(s // tm, nf), name=name,
        in_specs=[pl.BlockSpec((HALO, d), _halo_index(tm)),
                  pl.BlockSpec((tm, d), lambda i, j: (i, 0)),
                  pl.BlockSpec((None, 1, d), lambda i, j: (l, 0, 0)),
                  pl.BlockSpec((None, d, tf), lambda i, j: (l, 0, j)),
                  pl.BlockSpec((None, d, tf), lambda i, j: (l, 0, j + nf)),
                  pl.BlockSpec((None, FFN_CONV, tf), lambda i, j: (l, 0, j)),
                  pl.BlockSpec((None, 1, tf), lambda i, j: (l, 0, j)),
                  pl.BlockSpec((None, tf, d), lambda i, j: (l, j, 0))],
        out_specs=pl.BlockSpec((tm, d), lambda i, j: (i, 0)),
        out_shape=jax.ShapeDtypeStruct((s, d), F32),
        scratch_shapes=[pltpu.VMEM((tm + HALO, d), BF16), pltpu.VMEM((tm + HALO, tf), F32)],
        compiler_params=_params(("parallel", "arbitrary"), nbytes),
    )(x, x, g_all, w_in_all, w_in_all, cw_all, cb_all, w_out_all)


GDN_STEP_CHUNKS = 8
NEUMANN_DOUBLINGS = int(math.log2(GDN_CHUNK)) - 1


def _gdn_kernel(q_ref, k_ref, v_ref, z_ref, ba_ref, alog_ref, dt_ref, ng_ref, o_ref, st_ref):
    c = GDN_CHUNK
    rows = q_ref.shape[0]
    nchunks = rows // c

    @pl.when(pl.program_id(0) == 0)
    def _():
        st_ref[...] = jnp.zeros_like(st_ref)

    ba = ba_ref[...]
    lane = lax.broadcasted_iota(jnp.int32, ba.shape, 1)
    row_in_chunk = lax.broadcasted_iota(jnp.int32, ba.shape, 0) % c
    bg = jnp.where(lane < GDN_HEADS, _sigmoid(ba), -jnp.exp(alog_ref[...]) * _softplus(ba + dt_ref[...]))
    gcs = bg
    shift = 1
    while shift < c:
        gcs = gcs + jnp.where(row_in_chunk >= shift, pltpu.roll(gcs, shift, 0), 0.0)
        shift *= 2
    egcs = jnp.exp(gcs)

    ri = lax.broadcasted_iota(jnp.int32, (c, c), 0)
    ci = lax.broadcasted_iota(jnp.int32, (c, c), 1)
    causal = ri >= ci
    strict = ri > ci
    eye = (ri == ci).astype(F32)
    ng = ng_ref[...]

    heads = range(GDN_HEADS)
    chains = [(n, h) for n in range(nchunks) for h in heads]
    rsl = lambda n: slice(n * c, (n + 1) * c)
    hsl = lambda h: slice(h * GDN_DK, (h + 1) * GDN_DK)
    gct = [jnp.concatenate([gcs[rsl(n)], jnp.zeros((c, V7X_LANES), F32)], axis=0).T for n in range(nchunks)]

    kf, beta, gc_col, egc, g_last, decay, kb = {}, {}, {}, {}, {}, {}, {}
    for n, h in chains:
        kf[n, h] = k_ref[rsl(n), hsl(h)].astype(F32)
        beta[n, h] = bg[rsl(n), h:h + 1]
        gc_col[n, h] = gcs[rsl(n), GDN_HEADS + h:GDN_HEADS + h + 1]
        egc[n, h] = egcs[rsl(n), GDN_HEADS + h:GDN_HEADS + h + 1]
        gc_row = gct[n][GDN_HEADS + h:GDN_HEADS + h + 1, 0:c]
        g_last[n, h] = gc_row[:, c - 1:c]
        decay[n, h] = jnp.where(causal, jnp.exp(jnp.minimum(gc_col[n, h] - gc_row, 0.0)), 0.0)
        kb[n, h] = kf[n, h] * beta[n, h]
    qk = {ch: _dot_nt(jnp.concatenate([q_ref[rsl(ch[0]), hsl(ch[1])], kb[ch].astype(BF16)], axis=0),
                      k_ref[rsl(ch[0]), hsl(ch[1])]) for ch in chains}
    attn = {ch: (qk[ch][0:c] * decay[ch]).astype(BF16) for ch in chains}
    lmat = {ch: jnp.where(strict, qk[ch][c:2 * c] * decay[ch], 0.0) for ch in chains}
    pm = {ch: eye - lmat[ch] for ch in chains}
    mpow = {ch: lmat[ch].astype(BF16) for ch in chains}
    for _ in range(NEUMANN_DOUBLINGS):
        mpow = {ch: _dot(mpow[ch], mpow[ch]).astype(BF16) for ch in chains}
        pm = {ch: pm[ch] + _dot(pm[ch].astype(BF16), mpow[ch]) for ch in chains}
    uw = {ch: _dot(pm[ch].astype(BF16),
                   jnp.concatenate([v_ref[rsl(ch[0]), hsl(ch[1])].astype(F32) * beta[ch], kb[ch] * egc[ch]],
                                   axis=1).astype(BF16)) for ch in chains}
    wq = {ch: jnp.concatenate([uw[ch][:, GDN_DV:],
                               q_ref[rsl(ch[0]), hsl(ch[1])].astype(F32) * egc[ch]], axis=0).astype(BF16)
          for ch in chains}
    k_dec = {ch: (kf[ch] * jnp.exp(g_last[ch] - gc_col[ch])).astype(BF16) for ch in chains}

    state = [st_ref[h] for h in heads]
    for n in range(nchunks):
        ws = [_dot(wq[n, h], state[h].astype(BF16)) for h in heads]
        v_new = [(uw[n, h][:, 0:GDN_DV] - ws[h][0:c]).astype(BF16) for h in heads]
        state = [state[h] * jnp.exp(g_last[n, h]) + _dot_tn(k_dec[n, h], v_new[h]) for h in heads]
        o = [ws[h][c:2 * c] + _dot(attn[n, h], v_new[h]) for h in heads]
        for h in heads:
            o_ref[rsl(n), hsl(h)] = (_rms(o[h], ng) * z_ref[rsl(n), hsl(h)].astype(F32)).astype(o_ref.dtype)
    for h in heads:
        st_ref[h] = state[h]


def _gdn_mixer(p, ba, alog_all, dt_all, ng_all, e, *, name):
    s = p.shape[0]
    rows = GDN_STEP_CHUNKS * GDN_CHUNK
    col = lambda b: pl.BlockSpec((rows, GDN_V), lambda i: (i, b))
    layer = lambda shape: pl.BlockSpec((None,) + shape, lambda i: (e, 0, 0))
    return pl.pallas_call(
        _gdn_kernel, grid=(s // rows,), name=name,
        in_specs=[col(0), col(1), col(2), col(3),
                  pl.BlockSpec((rows, V7X_LANES), lambda i: (i, 0)),
                  layer((1, V7X_LANES)), layer((1, V7X_LANES)), layer((1, GDN_DV))],
        out_specs=pl.BlockSpec((rows, GDN_V), lambda i: (i, 0)),
        out_shape=jax.ShapeDtypeStruct((s, GDN_V), BF16),
        scratch_shapes=[pltpu.VMEM((GDN_HEADS, GDN_DK, GDN_DV), F32)],
        compiler_params=_params(("arbitrary",), 16 * rows * GDN_V * 4),
    )(p, p, p, p, ba, alog_all, dt_all, ng_all)


def _lru_kernel(xc_ref, gate_ref, wr_ref, br_ref, wi_ref, bi_ref, lam_ref, o_ref, a_ref, b_ref, carry_ref):
    tt = xc_ref.shape[0]

    @pl.when(pl.program_id(0) == 0)
    def _():
        carry_ref[...] = jnp.zeros_like(carry_ref)

    for blk in range(LRU_BLOCKS):
        sl = slice(blk * LRU_BLOCK, (blk + 1) * LRU_BLOCK)
        xb = xc_ref[:, sl]
        r = _sigmoid_tanh(_dot(xb, wr_ref[blk]) + br_ref[:, sl])
        i = _sigmoid_tanh(_dot(xb, wi_ref[blk]) + bi_ref[:, sl])
        log_a = (-LRU_C) * r * _softplus(-lam_ref[:, sl])
        a = jnp.exp(log_a)
        one_minus_a2 = -jnp.tanh(log_a) * (1.0 + a * a)
        a_ref[:, sl] = a
        b_ref[:, sl] = jnp.sqrt(one_minus_a2) * (i * xb.astype(F32))

    row = lax.broadcasted_iota(jnp.int32, (V7X_SUBLANES, LRU_WIDTH), 0)

    def group(gi, hprev):
        r0 = pl.multiple_of(gi * V7X_SUBLANES, V7X_SUBLANES)
        a = a_ref[pl.ds(r0, V7X_SUBLANES), :]
        b = b_ref[pl.ds(r0, V7X_SUBLANES), :]
        shift = 1
        while shift < V7X_SUBLANES:
            keep = row >= shift
            b = b + a * jnp.where(keep, pltpu.roll(b, shift, 0), 0.0)
            a = a * jnp.where(keep, pltpu.roll(a, shift, 0), 1.0)
            shift *= 2
        hcur = b + a * hprev
        b_ref[pl.ds(r0, V7X_SUBLANES), :] = hcur
        return jnp.broadcast_to(hcur[V7X_SUBLANES - 1:V7X_SUBLANES, :], hcur.shape)

    carry_ref[...] = lax.fori_loop(0, tt // V7X_SUBLANES, group, carry_ref[...])
    o_ref[...] = (b_ref[...] * gate_ref[...].astype(F32)).astype(o_ref.dtype)


def _lru_mixer(p, wr_all, br_all, wi_all, bi_all, lam_all, e, *, tt=512, name):
    s = p.shape[0]
    tt = min(tt, s)
    vec = pl.BlockSpec((None, 1, LRU_WIDTH), lambda i: (e, 0, 0))
    mat = pl.BlockSpec((None, LRU_BLOCKS, LRU_BLOCK, LRU_BLOCK), lambda i: (e, 0, 0, 0))
    return pl.pallas_call(
        _lru_kernel, grid=(s // tt,), name=name,
        in_specs=[pl.BlockSpec((tt, LRU_WIDTH), lambda i: (i, 4)),
                  pl.BlockSpec((tt, LRU_WIDTH), lambda i: (i, 5)),
                  mat, vec, mat, vec, vec],
        out_specs=pl.BlockSpec((tt, LRU_WIDTH), lambda i: (i, 0)),
        out_shape=jax.ShapeDtypeStruct((s, LRU_WIDTH), BF16),
        scratch_shapes=[pltpu.VMEM((tt, LRU_WIDTH), F32),
                        pltpu.VMEM((tt, LRU_WIDTH), F32),
                        pltpu.VMEM((V7X_SUBLANES, LRU_WIDTH), F32)],
        compiler_params=_params(("arbitrary",), 10 * tt * LRU_WIDTH * 4),
    )(p, p, wr_all, br_all, wi_all, bi_all, lam_all)


def _rope_kernel(pos_ref, inv_ref, cos_ref, sin_ref):
    ang = pos_ref[...].astype(F32) * inv_ref[...]
    cos_ref[...] = jnp.cos(ang)
    sin_ref[...] = jnp.sin(ang)


def _rope_tables(pos_col, inv_lanes, *, tm=1024):
    s = pos_col.shape[0]
    tm = min(tm, s)
    out = jax.ShapeDtypeStruct((s, V7X_LANES), F32)
    o_spec = pl.BlockSpec((tm, V7X_LANES), lambda i: (i, 0))
    return pl.pallas_call(
        _rope_kernel, grid=(s // tm,), name="rope_tables",
        in_specs=[pl.BlockSpec((tm, 1), lambda i: (i, 0)), pl.BlockSpec((1, V7X_LANES), lambda i: (0, 0))],
        out_specs=[o_spec, o_spec], out_shape=[out, out],
        compiler_params=_params(("parallel",), 8 * tm * V7X_LANES * 4),
    )(pos_col, inv_lanes)


SWA_PAIR = 2 * V7X_LANES
SWA_Q_PAIRS = SWA_Q // SWA_PAIR
SWA_KV_PAIRS = SWA_KV // SWA_PAIR
SWA_GROUPS_PER_PAIR = V7X_LANES // SWA_HALF


def _swa_kernel(sink_ref, qkv_ref, cos_ref, sin_ref, qg_ref, kg_ref, seg_ref, o_ref, k_ref, v_ref):
    blk = SWA_BLOCK
    first = pl.program_id(0) == 0

    @pl.when(first)
    def _():
        k_ref[0:blk, :] = jnp.zeros((blk, SWA_KV), BF16)
        v_ref[0:blk, :] = jnp.zeros((blk, SWA_KV), BF16)

    @pl.when(jnp.logical_not(first))
    def _():
        k_ref[0:blk, :] = k_ref[blk:2 * blk, :]
        v_ref[0:blk, :] = v_ref[blk:2 * blk, :]

    cos = cos_ref[...]
    sin = sin_ref[...]
    seg = seg_ref[...]

    def norm_rope(col0, gain_ref):
        a = qkv_ref[:, col0:col0 + V7X_LANES]
        b = qkv_ref[:, col0 + V7X_LANES:col0 + SWA_PAIR]
        sq = a * a + b * b
        hi = sq.astype(BF16)
        lo = (sq - hi.astype(F32)).astype(BF16)
        ss = _dot(hi, seg) + _dot(lo, seg)
        scale = lax.rsqrt(ss * (1.0 / SWA_HD) + EPS)
        a = a * scale * gain_ref[:, 0:V7X_LANES]
        b = b * scale * gain_ref[:, V7X_LANES:SWA_PAIR]
        return a * cos - b * sin, b * cos + a * sin

    for u in range(SWA_KV_PAIRS):
        ka, kb = norm_rope(SWA_Q + u * SWA_PAIR, kg_ref)
        k_ref[blk:2 * blk, u * SWA_PAIR:u * SWA_PAIR + V7X_LANES] = ka.astype(BF16)
        k_ref[blk:2 * blk, u * SWA_PAIR + V7X_LANES:(u + 1) * SWA_PAIR] = kb.astype(BF16)
    v_ref[blk:2 * blk, :] = qkv_ref[:, SWA_Q + SWA_KV:SWA_Q + 2 * SWA_KV].astype(BF16)

    qi = lax.broadcasted_iota(jnp.int32, (blk, 2 * blk), 0)
    kj = lax.broadcasted_iota(jnp.int32, (blk, 2 * blk), 1)
    lo_key = jnp.where(first, blk, 0)
    valid = (kj > qi) & (kj <= qi + SWA_WINDOW) & (kj >= lo_key)
    lane_group = lax.broadcasted_iota(jnp.int32, (blk, SWA_PAIR), 1) % V7X_LANES // SWA_HALF

    pairs = range(SWA_Q_PAIRS)
    groups = range(SWA_GROUPS_PER_PAIR)
    kv_pair = lambda t: t // SWA_GROUPS_PER_PAIR
    kv_cols = lambda t: slice(kv_pair(t) * SWA_PAIR, (kv_pair(t) + 1) * SWA_PAIR)
    qs = []
    for t in pairs:
        qa, qb = norm_rope(t * SWA_PAIR, qg_ref)
        qp = jnp.concatenate([qa, qb], axis=1) * (SWA_HD ** -0.5 * LOG2E)
        qs.append(jnp.concatenate([jnp.where(lane_group == gl, qp, 0.0) for gl in groups],
                                  axis=0).astype(BF16))
    sc_all = [_dot_nt(qs[t], k_ref[:, kv_cols(t)]) for t in pairs]
    ps, invs = [], []
    for t in pairs:
        pt, it = [], []
        for gl in groups:
            head = SWA_GROUP * (SWA_GROUPS_PER_PAIR * kv_pair(t) + gl) + t % SWA_GROUPS_PER_PAIR
            sink = sink_ref[head] * LOG2E
            sc = jnp.where(valid, sc_all[t][gl * blk:(gl + 1) * blk], -jnp.inf)
            m = jnp.maximum(jnp.max(sc, axis=-1, keepdims=True), sink)
            p = jnp.exp2(sc - m)
            it.append(1.0 / (jnp.sum(p, axis=-1, keepdims=True) + jnp.exp2(sink - m)))
            pt.append(p.astype(BF16))
        ps.append(jnp.concatenate(pt, axis=0))
        invs.append(it)
    pv = [_dot(ps[t], v_ref[:, kv_cols(t)]) for t in pairs]
    for t in pairs:
        out = jnp.zeros((blk, SWA_PAIR), F32)
        for gl in groups:
            out = jnp.where(lane_group == gl, pv[t][gl * blk:(gl + 1) * blk] * invs[t][gl], out)
        o_ref[:, t * SWA_PAIR:(t + 1) * SWA_PAIR] = out.astype(o_ref.dtype)


def _swa_mixer(qkv, sinks_all, cos, sin, qg_all, kg_all, seg, o, *, name):
    s, n = qkv.shape
    blk = SWA_BLOCK
    layer = lambda shape: pl.BlockSpec((None,) + shape, lambda i: (o, 0, 0))
    return pl.pallas_call(
        _swa_kernel, grid=(s // blk,), name=name,
        in_specs=[pl.BlockSpec(memory_space=pltpu.SMEM),
                  pl.BlockSpec((blk, n), lambda i: (i, 0)),
                  pl.BlockSpec((blk, V7X_LANES), lambda i: (i, 0)),
                  pl.BlockSpec((blk, V7X_LANES), lambda i: (i, 0)),
                  layer((1, SWA_PAIR)), layer((1, SWA_PAIR)),
                  pl.BlockSpec((V7X_LANES, V7X_LANES), lambda i: (0, 0))],
        out_specs=pl.BlockSpec((blk, SWA_Q), lambda i: (i, 0)),
        out_shape=jax.ShapeDtypeStruct((s, SWA_Q), BF16),
        scratch_shapes=[pltpu.VMEM((2 * blk, SWA_KV), BF16), pltpu.VMEM((2 * blk, SWA_KV), BF16)],
        compiler_params=_params(("arbitrary",), 8 * blk * n * 4),
    )(sinks_all[o], qkv, cos, sin, qg_all, kg_all, seg)


def _swa_q_src(t, half, gl):
    head = SWA_GROUPS_PER_PAIR * SWA_GROUP * (t // SWA_GROUPS_PER_PAIR) + SWA_GROUP * gl + t % SWA_GROUPS_PER_PAIR
    return head * SWA_HD + half * SWA_HALF


def _swa_kv_src(u, half, gl):
    return (SWA_GROUPS_PER_PAIR * u + gl) * SWA_HD + half * SWA_HALF


def _swa_in_prep_kernel(w_ref, o_ref):
    groups = range(SWA_GROUPS_PER_PAIR)

    def tile(srcs):
        return jnp.concatenate([w_ref[:, s:s + SWA_HALF] for s in srcs], axis=1).astype(BF16)

    for t in range(SWA_Q_PAIRS):
        for half in range(2):
            c0 = t * SWA_PAIR + half * V7X_LANES
            o_ref[:, c0:c0 + V7X_LANES] = tile([_swa_q_src(t, half, gl) for gl in groups])
    for base in (SWA_Q, SWA_Q + SWA_KV):
        for u in range(SWA_KV_PAIRS):
            for half in range(2):
                c0 = base + u * SWA_PAIR + half * V7X_LANES
                o_ref[:, c0:c0 + V7X_LANES] = tile([base + _swa_kv_src(u, half, gl) for gl in groups])


def _swa_in_prep(w, *, tr=256):
    n_layers, d, n = w.shape
    spec = pl.BlockSpec((None, tr, n), lambda e, i: (e, i, 0))
    return pl.pallas_call(
        _swa_in_prep_kernel, grid=(n_layers, d // tr), name="swa_in_prep",
        in_specs=[spec], out_specs=spec, out_shape=jax.ShapeDtypeStruct(w.shape, BF16),
        compiler_params=_params(("parallel", "parallel"), 2 * tr * n * 6),
    )(w)


def _swa_out_prep_kernel(w_ref, o_ref):
    for t in range(SWA_Q_PAIRS):
        for half in range(2):
            for gl in range(SWA_GROUPS_PER_PAIR):
                new = t * SWA_PAIR + half * V7X_LANES + gl * SWA_HALF
                src = _swa_q_src(t, half, gl)
                o_ref[new:new + SWA_HALF, :] = w_ref[src:src + SWA_HALF, :].astype(BF16)


def _swa_out_prep(w, *, tc=1024):
    n_layers, k, d = w.shape
    spec = pl.BlockSpec((None, k, tc), lambda e, j: (e, 0, j))
    return pl.pallas_call(
        _swa_out_prep_kernel, grid=(n_layers, d // tc), name="swa_out_prep",
        in_specs=[spec], out_specs=spec, out_shape=jax.ShapeDtypeStruct(w.shape, BF16),
        compiler_params=_params(("parallel", "parallel"), 2 * k * tc * 6),
    )(w)


def _gain_lanes(g):
    n = g.shape[0]
    g = jnp.broadcast_to(g.reshape(n, 2, 1, SWA_HALF), (n, 2, SWA_GROUPS_PER_PAIR, SWA_HALF))
    return g.reshape(n, 1, SWA_PAIR).astype(F32)


def _lane_rows(vals, offset):
    n, m = vals.shape
    return jnp.pad(vals.astype(F32), ((0, 0), (offset, V7X_LANES - offset - m))).reshape(n, 1, V7X_LANES)


def kernel(x, mem, positions, norm_mix, norm_cross, norm_mem, norm_ffn, xq_w, xkv_w, xo_w, xq_norm, xk_norm, ffn_in_w, ffn_conv_w, ffn_conv_b, ffn_out_w, hyb_in_w, hyb_out_w, gdn_conv_w, gdn_a_log, gdn_dt_bias, gdn_norm, lru_conv_w, lru_conv_b, lru_wr, lru_br, lru_wi, lru_bi, lru_lambda, swa_in_w, swa_out_w, swa_q_norm, swa_k_norm, swa_sinks):
    bsz, s, d = x.shape
    assert bsz == 1 and d == D_MODEL and s % 1024 == 0
    xs = x[0]
    rows3 = lambda v: v.reshape(v.shape[0], 1, v.shape[1]).astype(F32)

    norm_mix3, norm_cross3, norm_ffn3 = rows3(norm_mix), rows3(norm_cross), rows3(norm_ffn)
    hyb_t = hyb_in_w.transpose(0, 2, 1).astype(BF16)
    hyb_out_b = hyb_out_w.astype(BF16)
    alog3, dt3 = _lane_rows(gdn_a_log, GDN_HEADS), _lane_rows(gdn_dt_bias, GDN_HEADS)
    gdn_norm3 = rows3(gdn_norm)
    lru_cb3, lru_br3, lru_bi3, lru_lam3 = rows3(lru_conv_b), rows3(lru_br), rows3(lru_bi), rows3(lru_lambda)
    lru_wr_b, lru_wi_b = lru_wr.astype(BF16), lru_wi.astype(BF16)
    swa_in_b = _swa_in_prep(swa_in_w)
    swa_out_b = _swa_out_prep(swa_out_w)
    swa_qg, swa_kg = _gain_lanes(swa_q_norm), _gain_lanes(swa_k_norm)
    lanes = jnp.arange(V7X_LANES)
    seg = (lanes[:, None] // SWA_HALF == lanes[None, :] // SWA_HALF).astype(BF16)
    xq_norm3 = rows3(xq_norm)
    ffn_in_b, ffn_out_b = ffn_in_w.astype(BF16), ffn_out_w.astype(BF16)
    ffn_cb3 = rows3(ffn_conv_b)

    inv = 1.0 / (ROPE_THETA ** (jnp.arange(0, SWA_HD, 2, dtype=F32) / SWA_HD))
    inv_lanes = jnp.tile(inv, SWA_GROUPS_PER_PAIR).reshape(1, V7X_LANES)
    cos, sin = _rope_tables(positions.reshape(s, 1), inv_lanes)
    kn_all, v_all = _mem_kv(mem[0], rows3(norm_mem), xkv_w, rows3(xk_norm))

    for l in range(DEPTH):
        if l % 2 == 0:
            e = l // 2
            p, ba = _hyb_in(xs, norm_mix3, hyb_t, gdn_conv_w, lru_conv_w, lru_cb3, l, e,
                            name=f"hyb_in_{l}")
            oa = _gdn_mixer(p, ba, alog3, dt3, gdn_norm3, e, name=f"gdn_{l}")
            ob = _lru_mixer(p, lru_wr_b, lru_br3, lru_wi_b, lru_bi3, lru_lam3, e, name=f"lru_{l}")
            mixed, w_mix, wl = [oa, ob], hyb_out_b, e
        else:
            o = l // 2
            qkv = _norm_matmul(xs, norm_mix3, swa_in_b, l, o, name=f"swa_in_{l}")
            att = _swa_mixer(qkv, swa_sinks.astype(F32), cos, sin, swa_qg, swa_kg, seg, o, name=f"swa_{l}")
            mixed, w_mix, wl = [att], swa_out_b, o
        xs = _mix_cross_block(mixed, w_mix, wl, xs, norm_cross3, xq_w, xq_norm3, kn_all, v_all, xo_w, l,
                              name=f"mix_cross_{l}")
        xs = _ffn_block(xs, norm_ffn3, ffn_in_b, ffn_conv_w, ffn_cb3, ffn_out_b, l, name=f"ffn_{l}")
    return xs[None]
```
